```python
import math
import jax
import jax.numpy as jnp
from jax import lax
import numpy as np

D_MODEL = 4096
BATCH = 4
SEQ = 2048
DEPTH = 1
DEC_BATCH = 32
DEC_SEQ = 1
PAST_LEN = 8192
PAGE_SIZE = 128

HEAD_DIM = 128
N_HEADS = D_MODEL // 2 // HEAD_DIM
N_KV_HEADS = N_HEADS // 4
KV_GROUP = N_HEADS // N_KV_HEADS
MOBA_BLOCK = 256
MOBA_TOP = 3
Q_CHUNK = 16
CONV_CH = D_MODEL // 4
CONV_WIDTH = 31
N_MEM = 256
MEM_HEADS = 4
MEM_HEAD_DIM = D_MODEL // 16
N_EXPERTS = 32
TOP_K = 4
D_FF = D_MODEL
SWIGLU_LIMIT = 7.0
SWIGLU_ALPHA = 1.702
EXPERT_BLOCK = 128
N_BRANCH = 3
EPS = 1e-6

Q_W = N_HEADS * HEAD_DIM
KV_W = N_KV_HEADS * HEAD_DIM
MQ_W = MEM_HEADS * MEM_HEAD_DIM
IN_SPLITS = (CONV_CH, 2 * CONV_CH, 2 * CONV_CH + Q_W, 2 * CONV_CH + Q_W + KV_W,
             2 * CONV_CH + Q_W + 2 * KV_W, 2 * CONV_CH + Q_W + 2 * KV_W + MQ_W)
N_IN = IN_SPLITS[-1] + N_BRANCH * D_MODEL

kernel_name = "conv_moba_memory_moe_decode_step"

F32 = jnp.float32


def _rmsnorm(x, g):
    xf = x.astype(F32)
    y = xf * lax.rsqrt(jnp.mean(xf * xf, axis=-1, keepdims=True) + EPS)
    return (y * g.astype(F32)).astype(x.dtype)


def _alibi_slopes():
    return jnp.asarray(2.0 ** (-8.0 * np.arange(1, N_HEADS + 1) / N_HEADS), dtype=F32)


def _conv_module(u, hist, dw_w, dw_b, ln_g, ln_b, w_o):
    full = jnp.concatenate([hist.astype(u.dtype), u], axis=1)
    c = lax.conv_general_dilated(full, dw_w[:, None, :].astype(u.dtype), window_strides=(1,),
                                 padding='VALID', dimension_numbers=('NWC', 'WIO', 'NWC'),
                                 feature_group_count=CONV_CH)
    cf = c.astype(F32) + dw_b.astype(F32)
    mu = jnp.mean(cf, axis=-1, keepdims=True)
    var = jnp.mean(jnp.square(cf - mu), axis=-1, keepdims=True)
    cn = (cf - mu) * lax.rsqrt(var + EPS) * ln_g.astype(F32) + ln_b.astype(F32)
    act = (cn * jax.nn.sigmoid(cn)).astype(u.dtype)
    return act @ w_o, full[:, full.shape[1] - (CONV_WIDTH - 1):]


def _moba_attend(q, k_all, v_all, q_pos):
    b, n_q = q.shape[:2]
    n_blk = k_all.shape[1] // MOBA_BLOCK
    kb = k_all.reshape(b, n_blk, MOBA_BLOCK, N_KV_HEADS, HEAD_DIM).transpose(0, 3, 1, 2, 4)
    vb = v_all.reshape(b, n_blk, MOBA_BLOCK, N_KV_HEADS, HEAD_DIM).transpose(0, 3, 1, 2, 4)
    head_kv = jnp.arange(N_HEADS) // KV_GROUP
    kmean = jnp.mean(kb.astype(F32), axis=3)[:, head_kv]
    n_sel = min(MOBA_TOP, n_blk)
    slopes = _alibi_slopes()
    scale = HEAD_DIM ** -0.5
    b_ix = jnp.arange(b)[:, None, None, None]
    h_ix = head_kv[None, None, :, None]
    blk_ids = jnp.arange(n_blk, dtype=jnp.int32)
    offs = jnp.arange(MOBA_BLOCK, dtype=jnp.int32)

    def attend_chunk(args):
        qc, pc = args
        qf = qc.astype(F32)
        own = pc // MOBA_BLOCK
        gate = jnp.einsum('bchd,bhnd->bchn', qf, kmean)
        fully_past = blk_ids[None, None, None, :] < own[None, :, None, None]
        gate = jnp.where(fully_past, gate, -jnp.inf)
        _, sel = lax.top_k(gate, n_sel)
        sel_ok = sel < own[None, :, None, None]
        own_b = jnp.broadcast_to(own[None, :, None, None], sel.shape[:-1] + (1,)).astype(sel.dtype)
        blocks = jnp.concatenate([sel, own_b], axis=-1)
        ok = jnp.concatenate([sel_ok, jnp.ones(own_b.shape, bool)], axis=-1)
        kg = kb[b_ix, h_ix, blocks]
        vg = vb[b_ix, h_ix, blocks]
        key_pos = blocks[..., None] * MOBA_BLOCK + offs
        dist = (pc[None, :, None, None, None] - key_pos).astype(F32)
        valid = ok[..., None] & (dist >= 0)
        s = jnp.einsum('bchd,bchskd->bchsk', qf, kg.astype(F32)) * scale \
            - slopes[None, None, :, None, None] * dist
        s = jnp.where(valid, s, -jnp.inf)
        p = jax.nn.softmax(s.reshape(s.shape[:3] + (-1,)), axis=-1).reshape(s.shape)
        return jnp.einsum('bchsk,bchskd->bchd', p, vg.astype(F32)).astype(qc.dtype)

    chunk = Q_CHUNK if n_q % Q_CHUNK == 0 else n_q
    n_chunks = n_q // chunk
    qs = q.reshape(b, n_chunks, chunk, N_HEADS, HEAD_DIM).transpose(1, 0, 2, 3, 4)
    ps = q_pos.reshape(n_chunks, chunk)
    out = lax.map(attend_chunk, (qs, ps))
    return out.transpose(1, 0, 2, 3, 4).reshape(b, n_q, Q_W)


def _mem_kv(mem, mem_norm_g, w_mem_kv, mk_norm_g):
    b, m, _ = mem.shape
    kv = _rmsnorm(mem, mem_norm_g) @ w_mem_kv
    mk, mv = jnp.split(kv, 2, axis=-1)
    mk = _rmsnorm(mk.reshape(b, m, MEM_HEADS, MEM_HEAD_DIM), mk_norm_g)
    return mk, mv.reshape(b, m, MEM_HEADS, MEM_HEAD_DIM)


def _mem_attend(qm, mk, mv):
    b, t = qm.shape[:2]
    s = jnp.einsum('bthd,bmhd->bhtm', qm.astype(F32), mk.astype(F32)) * MEM_HEAD_DIM ** -0.5
    p = jax.nn.softmax(s, axis=-1)
    o = jnp.einsum('bhtm,bmhd->bthd', p, mv.astype(F32))
    return o.astype(qm.dtype).reshape(b, t, MQ_W)


def _moe_ffn(h, w_router, b_router, w_gu, b_gu, w_down, b_down):
    n_tok = h.shape[0]
    logits = jnp.dot(h.astype(F32), w_router.astype(F32)) + b_router.astype(F32)
    top_val, top_idx = lax.top_k(logits, TOP_K)
    gate = jax.nn.softmax(top_val, axis=-1)
    n_asg = n_tok * TOP_K
    e_flat = top_idx.reshape(n_asg)
    tok_flat = jnp.repeat(jnp.arange(n_tok, dtype=jnp.int32), TOP_K)
    order = jnp.argsort(e_flat)
    e_s = e_flat[order]
    tok_s = tok_flat[order]
    g_s = gate.reshape(n_asg)[order]
    counts = jnp.bincount(e_flat, length=N_EXPERTS)
    starts = jnp.cumsum(counts) - counts
    padded = (counts + EXPERT_BLOCK - 1) // EXPERT_BLOCK * EXPERT_BLOCK
    pend = jnp.cumsum(padded)
    pstart = pend - padded
    dest = pstart[e_s] + jnp.arange(n_asg) - starts[e_s]
    n_blocks = (n_asg + N_EXPERTS * (EXPERT_BLOCK - 1) + EXPERT_BLOCK - 1) // EXPERT_BLOCK
    buf = jnp.zeros((n_blocks * EXPERT_BLOCK, h.shape[1]), h.dtype).at[dest].set(h[tok_s])
    blk_e = jnp.minimum(jnp.searchsorted(pend, jnp.arange(n_blocks) * EXPERT_BLOCK, side='right'),
                        N_EXPERTS - 1)

    def expert_block(args):
        xb, e = args
        gu = (xb @ w_gu[e]).astype(F32) + b_gu[e].astype(F32)
        glu, lin = jnp.split(gu, 2, axis=-1)
        glu = jnp.minimum(glu, SWIGLU_LIMIT)
        lin = jnp.clip(lin, -SWIGLU_LIMIT, SWIGLU_LIMIT)
        act = (glu * jax.nn.sigmoid(SWIGLU_ALPHA * glu) * (lin + 1.0)).astype(xb.dtype)
        return act @ w_down[e] + b_down[e]

    out_buf = lax.map(expert_block, (buf.reshape(n_blocks, EXPERT_BLOCK, -1), blk_e))
    out_buf = out_buf.reshape(n_blocks * EXPERT_BLOCK, -1)
    contrib = out_buf[dest].astype(F32) * g_s[:, None]
    return jax.ops.segment_sum(contrib, tok_s, num_segments=n_tok).astype(h.dtype)


def _decoder_layer(x, conv_hist, k_hist, v_hist, mem_k, mem_v, lw):
    b, t, _ = x.shape
    past = k_hist.shape[1]
    xn = _rmsnorm(x, lw['norm_mix_g'])
    proj = xn @ lw['w_in']
    u_a, u_b, q, k, v, qm, gates = jnp.split(proj, IN_SPLITS, axis=-1)
    u = u_a * jax.nn.sigmoid(u_b)
    conv_out, conv_new = _conv_module(u, conv_hist, lw['conv_dw_w'], lw['conv_dw_b'],
                                      lw['conv_ln_g'], lw['conv_ln_b'], lw['w_conv_o'])
    q = _rmsnorm(q.reshape(b, t, N_HEADS, HEAD_DIM), lw['q_norm_g'])
    k = _rmsnorm(k.reshape(b, t, N_KV_HEADS, HEAD_DIM), lw['k_norm_g'])
    v = v.reshape(b, t, N_KV_HEADS, HEAD_DIM)
    n_tot = past + t
    l_pad = -(-n_tot // MOBA_BLOCK) * MOBA_BLOCK
    pad = ((0, 0), (0, l_pad - n_tot), (0, 0), (0, 0))
    k_all = jnp.pad(jnp.concatenate([k_hist.astype(k.dtype), k], axis=1), pad)
    v_all = jnp.pad(jnp.concatenate([v_hist.astype(v.dtype), v], axis=1), pad)
    pos = past + jnp.arange(t, dtype=jnp.int32)
    attn_out = _moba_attend(q, k_all, v_all, pos) @ lw['w_attn_o']
    qm = _rmsnorm(qm.reshape(b, t, MEM_HEADS, MEM_HEAD_DIM), lw['mq_norm_g'])
    mem_out = _mem_attend(qm, mem_k, mem_v) @ lw['w_mem_o']
    g = jax.nn.sigmoid(gates.astype(F32)).reshape(b, t, N_BRANCH, D_MODEL).astype(x.dtype)
    merged = g[:, :, 0] * conv_out + g[:, :, 1] * attn_out + g[:, :, 2] * mem_out
    h = x + merged @ lw['w_out']
    hn = _rmsnorm(h, lw['norm_ffn_g']).reshape(b * t, D_MODEL)
    y = h + _moe_ffn(hn, lw['w_router'], lw['b_router'], lw['w_gu'], lw['b_gu'],
                     lw['w_down'], lw['b_down']).reshape(b, t, D_MODEL)
    return y, k, v, conv_new


def setup_inputs(seed: int = 0) -> dict:
    key = jax.random.key(seed)
    ks = iter(jax.random.split(key, 40))

    def nrm(shape, scale):
        return jax.random.normal(next(ks), shape, F32) * scale

    def gain(shape):
        return 1.0 + nrm(shape, 0.05)

    n_pages = PAST_LEN // PAGE_SIZE
    n_pool = DEC_BATCH * n_pages * 5 // 4
    page_table = jax.random.permutation(next(ks), n_pool)[:DEC_BATCH * n_pages]
    page_table = page_table.reshape(DEC_BATCH, n_pages).astype(jnp.int32)
    return {
        'x_prompt': nrm((BATCH, SEQ, D_MODEL), 1.0),
        'x_sample': nrm((DEC_BATCH, DEC_SEQ, D_MODEL), 1.0),
        'cache_k': nrm((DEPTH, n_pool, PAGE_SIZE, N_KV_HEADS, HEAD_DIM), 1.0),
        'cache_v': nrm((DEPTH, n_pool, PAGE_SIZE, N_KV_HEADS, HEAD_DIM), 1.0),
        'cache_mem_k': nrm((DEPTH, DEC_BATCH, N_MEM, MEM_HEADS, MEM_HEAD_DIM), 1.0),
        'cache_mem_v': nrm((DEPTH, DEC_BATCH, N_MEM, MEM_HEADS, MEM_HEAD_DIM), 1.0),
        'state_conv': nrm((DEPTH, DEC_BATCH, CONV_WIDTH - 1, CONV_CH), 0.5),
        'page_table': page_table,
        'mem_prompt': nrm((BATCH, N_MEM, D_MODEL), 1.0),
        'norm_mix_g': gain((DEPTH, D_MODEL)),
        'w_in': nrm((DEPTH, D_MODEL, N_IN), D_MODEL ** -0.5),
        'q_norm_g': gain((DEPTH, HEAD_DIM)),
        'k_norm_g': gain((DEPTH, HEAD_DIM)),
        'w_attn_o': nrm((DEPTH, Q_W, D_MODEL), Q_W ** -0.5),
        'conv_dw_w': nrm((DEPTH, CONV_WIDTH, CONV_CH), CONV_WIDTH ** -0.5),
        'conv_dw_b': nrm((DEPTH, CONV_CH), 0.02),
        'conv_ln_g': gain((DEPTH, CONV_CH)),
        'conv_ln_b': nrm((DEPTH, CONV_CH), 0.02),
        'w_conv_o': nrm((DEPTH, CONV_CH, D_MODEL), CONV_CH ** -0.5),
        'mem_norm_g': gain((DEPTH, D_MODEL)),
        'w_mem_kv': nrm((DEPTH, D_MODEL, 2 * MQ_W), D_MODEL ** -0.5),
        'mq_norm_g': gain((DEPTH, MEM_HEAD_DIM)),
        'mk_norm_g': gain((DEPTH, MEM_HEAD_DIM)),
        'w_mem_o': nrm((DEPTH, MQ_W, D_MODEL), MQ_W ** -0.5),
        'w_out': nrm((DEPTH, D_MODEL, D_MODEL), D_MODEL ** -0.5),
        'norm_ffn_g': gain((DEPTH, D_MODEL)),
        'w_router': nrm((DEPTH, D_MODEL, N_EXPERTS), D_MODEL ** -0.5),
        'b_router': nrm((DEPTH, N_EXPERTS), 0.01),
        'w_gu': nrm((DEPTH, N_EXPERTS, D_MODEL, 2 * D_FF), D_MODEL ** -0.5),
        'b_gu': nrm((DEPTH, N_EXPERTS, 2 * D_FF), 0.02),
        'w_down': nrm((DEPTH, N_EXPERTS, D_FF, D_MODEL), D_FF ** -0.5),
        'b_down': nrm((DEPTH, N_EXPERTS, D_MODEL), 0.02),
    }


def reference(x_prompt, x_sample, cache_k, cache_v, cache_mem_k, cache_mem_v, state_conv, page_table,
              mem_prompt, norm_mix_g, w_in, q_norm_g, k_norm_g, w_attn_o, conv_dw_w, conv_dw_b,
              conv_ln_g, conv_ln_b, w_conv_o, mem_norm_g, w_mem_kv, mq_norm_g, mk_norm_g, w_mem_o,
              w_out, norm_ffn_g, w_router, b_router, w_gu, b_gu, w_down, b_down):
    hp = x_prompt
    hs = x_sample
    bp = x_prompt.shape[0]
    bs = x_sample.shape[0]
    kp_l, vp_l, ks_l, vs_l, mkp_l, mvp_l, cp_l, cs_l = [], [], [], [], [], [], [], []
    for l in range(DEPTH):
        lw = {
            'norm_mix_g': norm_mix_g[l], 'w_in': w_in[l], 'q_norm_g': q_norm_g[l], 'k_norm_g': k_norm_g[l],
            'w_attn_o': w_attn_o[l], 'conv_dw_w': conv_dw_w[l], 'conv_dw_b': conv_dw_b[l],
            'conv_ln_g': conv_ln_g[l], 'conv_ln_b': conv_ln_b[l], 'w_conv_o': w_conv_o[l],
            'mq_norm_g': mq_norm_g[l], 'w_mem_o': w_mem_o[l], 'w_out': w_out[l],
            'norm_ffn_g': norm_ffn_g[l], 'w_router': w_router[l], 'b_router': b_router[l],
            'w_gu': w_gu[l], 'b_gu': b_gu[l], 'w_down': w_down[l], 'b_down': b_down[l],
        }
        mk_p, mv_p = _mem_kv(mem_prompt, mem_norm_g[l], w_mem_kv[l], mk_norm_g[l])
        zero_hist = jnp.zeros((bp, CONV_WIDTH - 1, CONV_CH), hp.dtype)
        empty_kv = jnp.zeros((bp, 0, N_KV_HEADS, HEAD_DIM), hp.dtype)
        hp, kp, vp, cp = _decoder_layer(hp, zero_hist, empty_kv, empty_kv, mk_p, mv_p, lw)
        k_past = cache_k[l][page_table].reshape(bs, -1, N_KV_HEADS, HEAD_DIM)
        v_past = cache_v[l][page_table].reshape(bs, -1, N_KV_HEADS, HEAD_DIM)
        hs, ksn, vsn, csn = _decoder_layer(hs, state_conv[l], k_past, v_past,
                                           cache_mem_k[l], cache_mem_v[l], lw)
        kp_l.append(kp)
        vp_l.append(vp)
        ks_l.append(ksn)
        vs_l.append(vsn)
        mkp_l.append(mk_p)
        mvp_l.append(mv_p)
        cp_l.append(cp)
        cs_l.append(csn)
    return (hp, hs, jnp.stack(kp_l), jnp.stack(vp_l), jnp.stack(ks_l), jnp.stack(vs_l),
            jnp.stack(mkp_l), jnp.stack(mvp_l), jnp.stack(cp_l), jnp.stack(cs_l))
```

```python
import functools

import numpy as np
import jax
import jax.numpy as jnp
from jax import lax
from jax.experimental import pallas as pl
from jax.experimental.pallas import tpu as pltpu

F32 = jnp.float32
BF16 = jnp.bfloat16
I32 = jnp.int32

EPS = 1e-6
MOBA_BLOCK = 256
MOBA_TOP = 3
TOP_K = 4
SWIGLU_LIMIT = 7.0
SWIGLU_ALPHA = 1.702
CONV_HALO = 32

V7X_VMEM_LIMIT_BYTES = 56 * 1024 * 1024
LANE = 128

MOE_SUB = 256
MOE_BLOCK_ROWS = 5 * MOE_SUB
KV_CHUNK_PAGES = 8


def _params(n_axes):
    return pltpu.CompilerParams(dimension_semantics=("arbitrary",) * n_axes,
                                vmem_limit_bytes=V7X_VMEM_LIMIT_BYTES)


def _split_bf16(a):
    hi = a.astype(BF16)
    lo = (a - hi.astype(F32)).astype(BF16)
    return hi, lo


def _dot(a, b, dims=(((1,), (0,)), ((), ()))):
    return lax.dot_general(a, b, dims, preferred_element_type=F32)


def _dot_hi(a, b, dims=(((1,), (0,)), ((), ()))):
    ah, al = _split_bf16(a)
    bh, bl = _split_bf16(b)
    return _dot(ah, bh, dims) + (_dot(ah, bl, dims) + _dot(al, bh, dims))


_NT = (((1,), (1,)), ((), ()))


def _rms(x, g):
    return x * lax.rsqrt(jnp.mean(x * x, axis=-1, keepdims=True) + EPS) * g


def _rmsnorm_kernel(x_ref, g_ref, o_ref):
    o_ref[...] = _rms(x_ref[...], g_ref[...]).astype(o_ref.dtype)


def _rmsnorm(x, g, out_dtype):
    m, d = x.shape
    tm = min(m, 256)
    assert m % tm == 0
    return pl.pallas_call(
        _rmsnorm_kernel,
        grid=(m // tm,),
        in_specs=[pl.BlockSpec((tm, d), lambda i: (i, 0)),
                  pl.BlockSpec((1, d), lambda i: (0, 0))],
        out_specs=pl.BlockSpec((tm, d), lambda i: (i, 0)),
        out_shape=jax.ShapeDtypeStruct((m, d), out_dtype),
        compiler_params=_params(1),
        name="rmsnorm",
    )(x, g.reshape(1, d))


def _mm_kernel(x_ref, w_ref, *rest, hi, has_res):
    o_ref = rest[-1]
    if hi:
        acc = _dot_hi(x_ref[...], w_ref[...])
    else:
        acc = _dot(x_ref[...], w_ref[...].astype(BF16))
    if has_res:
        acc = rest[0][...] + acc
    o_ref[...] = acc


def _mm(x, w, *, residual=None, hi=False, tn=512):
    m, k = x.shape
    n = w.shape[1]
    tm = min(m, 1024)
    assert m % tm == 0 and n % tn == 0
    in_specs = [pl.BlockSpec((tm, k), lambda i, j: (i, 0)),
                pl.BlockSpec((k, tn), lambda i, j: (0, j))]
    args = [x, w]
    if residual is not None:
        in_specs.append(pl.BlockSpec((tm, tn), lambda i, j: (i, j)))
        args.append(residual)
    return pl.pallas_call(
        functools.partial(_mm_kernel, hi=hi, has_res=residual is not None),
        grid=(m // tm, n // tn),
        in_specs=in_specs,
        out_specs=pl.BlockSpec((tm, tn), lambda i, j: (i, j)),
        out_shape=jax.ShapeDtypeStruct((m, n), F32),
        compiler_params=_params(2),
        name="matmul",
    )(*args)


def _headnorm_kernel(x_ref, g_ref, o_ref, *mean_ref, hd, nh):
    for h in range(nh):
        y = _rms(x_ref[:, h * hd:(h + 1) * hd], g_ref[...])
        o_ref[:, h * hd:(h + 1) * hd] = y
        if mean_ref:
            mean_ref[0][0, :, h * hd:(h + 1) * hd] = jnp.mean(y, axis=0, keepdims=True)


def _headnorm(x, col_off, width, g, *, rows_per_block, with_mean):
    m = x.shape[0]
    hd = g.shape[-1]
    tm = min(m, rows_per_block)
    assert m % tm == 0 and col_off % width == 0 and width % hd == 0
    cb = col_off // width
    out_shape = [jax.ShapeDtypeStruct((m, width), F32)]
    out_specs = [pl.BlockSpec((tm, width), lambda i: (i, 0))]
    if with_mean:
        out_shape.append(jax.ShapeDtypeStruct((m // tm, 1, width), F32))
        out_specs.append(pl.BlockSpec((1, 1, width), lambda i: (i, 0, 0)))
    res = pl.pallas_call(
        functools.partial(_headnorm_kernel, hd=hd, nh=width // hd),
        grid=(m // tm,),
        in_specs=[pl.BlockSpec((tm, width), lambda i: (i, cb)),
                  pl.BlockSpec((1, hd), lambda i: (0, 0))],
        out_specs=out_specs,
        out_shape=out_shape,
        compiler_params=_params(1),
        name="headnorm",
    )(x, g.reshape(1, hd))
    return res if with_mean else res[0]


def _conv_prompt_kernel(ua_ref, ub_ref, w_ref, b_ref, g_ref, beta_ref, act_ref, new_ref,
                        full_ref, conv_ref, *, tt, nt, width, ch):
    t = pl.program_id(1)
    hist = width - 1
    lead = CONV_HALO - hist

    @pl.when(t == 0)
    def _():
        full_ref[0:CONV_HALO, :] = jnp.zeros((CONV_HALO, ch), F32)

    full_ref[CONV_HALO:CONV_HALO + tt, :] = ua_ref[...] * jax.nn.sigmoid(ub_ref[...])

    rc, cc = 32, 256
    for r in range(tt // rc):
        for c in range(ch // cc):
            cs = slice(c * cc, (c + 1) * cc)
            acc = jnp.zeros((rc, cc), F32)
            for w in range(width):
                r0 = lead + w + r * rc
                acc = acc + full_ref[r0:r0 + rc, cs] * w_ref[w:w + 1, cs]
            conv_ref[r * rc:(r + 1) * rc, cs] = acc + b_ref[:, cs]

    for r in range(tt // rc):
        cf = conv_ref[r * rc:(r + 1) * rc, :]
        mu = jnp.mean(cf, axis=-1, keepdims=True)
        d = cf - mu
        var = jnp.mean(d * d, axis=-1, keepdims=True)
        cn = d * lax.rsqrt(var + EPS) * g_ref[...] + beta_ref[...]
        act_ref[r * rc:(r + 1) * rc, :] = (cn * jax.nn.sigmoid(cn)).astype(act_ref.dtype)

    tail = full_ref[tt + lead:tt + CONV_HALO, :]
    full_ref[lead:CONV_HALO, :] = tail

    @pl.when(t == nt - 1)
    def _():
        new_ref[0] = tail


def _conv_prompt(proj, b, s, ch, dw_w, dw_b, ln_g, ln_b):
    width = dw_w.shape[0]
    tt = 256
    assert s % tt == 0 and ch % 256 == 0 and width - 1 <= CONV_HALO
    nt = s // tt
    row = lambda bi, ti: (bi * nt + ti, 0)
    vec = pl.BlockSpec((1, ch), lambda bi, ti: (0, 0))
    return pl.pallas_call(
        functools.partial(_conv_prompt_kernel, tt=tt, nt=nt, width=width, ch=ch),
        grid=(b, nt),
        in_specs=[pl.BlockSpec((tt, ch), row),
                  pl.BlockSpec((tt, ch), lambda bi, ti: (bi * nt + ti, 1)),
                  pl.BlockSpec((width, ch), lambda bi, ti: (0, 0)),
                  vec, vec, vec],
        out_specs=[pl.BlockSpec((tt, ch), row),
                   pl.BlockSpec((1, width - 1, ch), lambda bi, ti: (bi, 0, 0))],
        out_shape=[jax.ShapeDtypeStruct((b * s, ch), BF16),
                   jax.ShapeDtypeStruct((b, width - 1, ch), F32)],
        scratch_shapes=[pltpu.VMEM((CONV_HALO + tt, ch), F32),
                        pltpu.VMEM((tt, ch), F32)],
        compiler_params=_params(2),
        name="conv_prompt",
    )(proj, proj, dw_w, dw_b.reshape(1, ch), ln_g.reshape(1, ch), ln_b.reshape(1, ch))


def _conv_sample_kernel(ua_ref, ub_ref, hist_ref, w_ref, b_ref, g_ref, beta_ref,
                        act_ref, new_ref, *, width):
    hist = width - 1
    u = ua_ref[0] * jax.nn.sigmoid(ub_ref[0])
    h = hist_ref[0]
    cf = (jnp.sum(h * w_ref[0:hist, :], axis=0, keepdims=True)
          + u * w_ref[hist:width, :] + b_ref[...])
    mu = jnp.mean(cf, axis=-1, keepdims=True)
    d = cf - mu
    var = jnp.mean(d * d, axis=-1, keepdims=True)
    cn = d * lax.rsqrt(var + EPS) * g_ref[...] + beta_ref[...]
    act_ref[0] = cn * jax.nn.sigmoid(cn)
    new_ref[0, 0:hist - 1, :] = hist_ref[0, 1:hist, :]
    new_ref[0, hist - 1:hist, :] = u


def _conv_sample(proj3, state, dw_w, dw_b, ln_g, ln_b):
    b, hist, ch = state.shape
    width = dw_w.shape[0]
    vec = pl.BlockSpec((1, ch), lambda bi: (0, 0))
    return pl.pallas_call(
        functools.partial(_conv_sample_kernel, width=width),
        grid=(b,),
        in_specs=[pl.BlockSpec((1, 1, ch), lambda bi: (bi, 0, 0)),
                  pl.BlockSpec((1, 1, ch), lambda bi: (bi, 0, 1)),
                  pl.BlockSpec((1, hist, ch), lambda bi: (bi, 0, 0)),
                  pl.BlockSpec((width, ch), lambda bi: (0, 0)),
                  vec, vec, vec],
        out_specs=[pl.BlockSpec((1, 1, ch), lambda bi: (bi, 0, 0)),
                   pl.BlockSpec((1, hist, ch), lambda bi: (bi, 0, 0))],
        out_shape=[jax.ShapeDtypeStruct((b, 1, ch), F32),
                   jax.ShapeDtypeStruct((b, hist, ch), F32)],
        compiler_params=_params(1),
        name="conv_sample",
    )(proj3, proj3, state, dw_w, dw_b.reshape(1, ch), ln_g.reshape(1, ch), ln_b.reshape(1, ch))


def _rank_select(gate, lane, n_valid, n_top, n_cols):
    neg = jnp.float32(-jnp.inf)
    gm = jnp.where(lane < n_valid, gate, neg)
    rank = jnp.zeros(gate.shape, F32)
    for m in range(n_cols):
        gcol = gm[:, m:m + 1]
        beats = (gcol > gm) | ((gcol == gm) & (m < lane))
        rank = rank + jnp.where(beats, 1.0, 0.0)
    return jnp.where((rank < n_top) & (lane < n_valid), 1.0, 0.0)


def _moba_prompt_kernel(slopes_ref, q_ref, k_ref, v_ref, km_ref, gq_ref, o_ref,
                        *, nb, blk, group, hd):
    kvh = pl.program_id(1)
    i = pl.program_id(2)
    scale = hd ** -0.5
    neg = jnp.float32(-jnp.inf)
    kb = k_ref[...].astype(BF16)
    vb = v_ref[...].astype(BF16)
    km = km_ref[0]
    row = lax.broadcasted_iota(I32, (blk, blk), 0)
    col = lax.broadcasted_iota(I32, (blk, blk), 1)
    tri = col <= row
    rc = (row - col).astype(F32)
    own = jnp.full((blk, blk), i, I32)
    lane = lax.broadcasted_iota(I32, (blk, nb), 1)
    for hh in range(group):
        qh = _rms(q_ref[:, hh * hd:(hh + 1) * hd], gq_ref[...])
        sel = _rank_select(_dot_hi(qh, km, _NT), lane, i, MOBA_TOP, nb)
        slope = slopes_ref[kvh * group + hh]
        s_all = _dot(qh.astype(BF16), kb, _NT) * scale
        parts = []
        mx = jnp.full((blk, 1), neg, F32)
        for j in range(nb):
            dist = rc + ((i - j) * blk).astype(F32)
            sj = s_all[:, j * blk:(j + 1) * blk] - slope * dist
            picked = jnp.broadcast_to(sel[:, j:j + 1], (blk, blk)) > 0.5
            allowed = (tri & (own == j)) | (picked & (own > j))
            sj = jnp.where(allowed, sj, neg)
            mx = jnp.maximum(mx, jnp.max(sj, axis=-1, keepdims=True))
            parts.append(sj)
        den = jnp.zeros((blk, 1), F32)
        acc = jnp.zeros((blk, hd), F32)
        for j in range(nb):
            p = jnp.exp(parts[j] - mx)
            den = den + jnp.sum(p, axis=-1, keepdims=True)
            acc = acc + _dot(p.astype(BF16), vb[j * blk:(j + 1) * blk, :])
        o_ref[:, hh * hd:(hh + 1) * hd] = (acc / den).astype(o_ref.dtype)


def _moba_prompt(proj, kn, kmean, gq, slopes, b, s, o_q, o_v, n_heads, n_kv):
    hd = gq.shape[-1]
    blk = MOBA_BLOCK
    nb = s // blk
    group = n_heads // n_kv
    gw = group * hd
    assert s % blk == 0 and o_q % gw == 0 and o_v % hd == 0
    return pl.pallas_call(
        functools.partial(_moba_prompt_kernel, nb=nb, blk=blk, group=group, hd=hd),
        grid_spec=pltpu.PrefetchScalarGridSpec(
            num_scalar_prefetch=1,
            grid=(b, n_kv, nb),
            in_specs=[pl.BlockSpec((blk, gw), lambda bi, g, i, sl: (bi * nb + i, o_q // gw + g)),
                      pl.BlockSpec((s, hd), lambda bi, g, i, sl: (bi, g)),
                      pl.BlockSpec((s, hd), lambda bi, g, i, sl: (bi, o_v // hd + g)),
                      pl.BlockSpec((1, nb, hd), lambda bi, g, i, sl: (bi, 0, g)),
                      pl.BlockSpec((1, hd), lambda bi, g, i, sl: (0, 0))],
            out_specs=pl.BlockSpec((blk, gw), lambda bi, g, i, sl: (bi * nb + i, g)),
        ),
        out_shape=jax.ShapeDtypeStruct((b * s, n_heads * hd), BF16),
        compiler_params=_params(3),
        name="moba_prompt",
    )(slopes, proj, kn, proj, kmean, gq.reshape(1, hd))


def _moba_sample_kernel(pt_ref, q_ref, kn_ref, vn_ref, gq_ref, slope_ref, ck_ref, cv_ref, o_ref,
                        buf, sem, s_ref, *, n_pages, page, n_kv, hd, blk):
    b = pl.program_id(0)
    n_heads = q_ref.shape[1]
    group = n_heads // n_kv
    ppc = KV_CHUNK_PAGES
    ch = ppc * page
    n_chunks = n_pages // ppc
    past = n_pages * page
    n_blk = past // blk
    bpc = ch // blk
    scale = hd ** -0.5
    neg = jnp.float32(-jnp.inf)

    def copies(step):
        src = ck_ref if step < n_chunks else cv_ref
        c = step % n_chunks
        slot = step % 2
        return [pltpu.make_async_copy(src.at[pt_ref[b, c * ppc + p]],
                                      buf.at[slot, pl.ds(p * page, page), :],
                                      sem.at[slot]) for p in range(ppc)]

    def start(step):
        for cp in copies(step):
            cp.start()

    def wait(step):
        for cp in copies(step):
            cp.wait()

    qn = _rms(q_ref[0], gq_ref[...])
    head_grp = lax.broadcasted_iota(I32, (n_heads, 1), 0) // group
    lane_b = lax.broadcasted_iota(I32, (n_heads, LANE), 1)
    gates = jnp.zeros((n_heads, LANE), F32)

    start(0)
    for c in range(n_chunks):
        start(c + 1)
        wait(c)
        slot = c % 2
        sc = jnp.zeros((n_heads, ch), F32)
        for g in range(n_kv):
            sg = _dot_hi(qn, buf[slot, :, g * hd:(g + 1) * hd], _NT)
            sc = jnp.where(head_grp == g, sg, sc)
        s_ref[:, c * ch:(c + 1) * ch] = sc
        for k in range(bpc):
            gsum = jnp.sum(sc[:, k * blk:(k + 1) * blk], axis=-1, keepdims=True) * (1.0 / blk)
            gates = jnp.where(lane_b == c * bpc + k, gsum, gates)

    sel = _rank_select(gates, lane_b, n_blk, min(MOBA_TOP, n_blk + 1), n_blk)
    self_all = _dot_hi(qn, kn_ref[0], _NT)
    grp_lane = lax.broadcasted_iota(I32, (n_heads, n_kv), 1)
    s_self = jnp.sum(jnp.where(grp_lane == head_grp, self_all, 0.0), axis=-1, keepdims=True) * scale
    slope = slope_ref[...]
    mx = s_self
    pos = lax.broadcasted_iota(I32, (n_heads, ch), 1)
    for c in range(n_chunks):
        dist = (past - c * ch - pos).astype(F32)
        sc = s_ref[:, c * ch:(c + 1) * ch] * scale - slope * dist
        keep = jnp.concatenate(
            [jnp.broadcast_to(
                jnp.sum(jnp.where(lane_b == c * bpc + k, sel, 0.0), axis=-1, keepdims=True),
                (n_heads, blk)) for k in range(bpc)], axis=-1)
        sc = jnp.where(keep > 0.5, sc, neg)
        s_ref[:, c * ch:(c + 1) * ch] = sc
        mx = jnp.maximum(mx, jnp.max(sc, axis=-1, keepdims=True))

    p_self = jnp.exp(s_self - mx)
    den = p_self
    acc = jnp.zeros((n_heads, n_kv * hd), F32)
    for c in range(n_chunks):
        step = n_chunks + c
        if step + 1 < 2 * n_chunks:
            start(step + 1)
        wait(step)
        p = jnp.exp(s_ref[:, c * ch:(c + 1) * ch] - mx)
        den = den + jnp.sum(p, axis=-1, keepdims=True)
        acc = acc + _dot_hi(p, buf[step % 2])
    out = jnp.zeros((n_heads, hd), F32)
    for g in range(n_kv):
        og = acc[:, g * hd:(g + 1) * hd] + p_self * vn_ref[0, g:g + 1, :]
        out = jnp.where(head_grp == g, og, out)
    o_ref[0] = out / den


def _moba_sample(page_table, q3, kn3, vn3, gq, slopes, cache_k, cache_v):
    b, n_heads, hd = q3.shape
    n_kv = kn3.shape[1]
    n_pages = page_table.shape[1]
    n_pool, page = cache_k.shape[0], cache_k.shape[1]
    kvw = n_kv * hd
    assert (n_pages * page) % MOBA_BLOCK == 0 and n_pages % KV_CHUNK_PAGES == 0
    assert (KV_CHUNK_PAGES * page) % MOBA_BLOCK == 0 and (n_pages * page) // MOBA_BLOCK <= LANE
    ch = KV_CHUNK_PAGES * page
    hbm = pl.BlockSpec(memory_space=pl.ANY)
    return pl.pallas_call(
        functools.partial(_moba_sample_kernel, n_pages=n_pages, page=page, n_kv=n_kv, hd=hd,
                          blk=MOBA_BLOCK),
        grid_spec=pltpu.PrefetchScalarGridSpec(
            num_scalar_prefetch=1,
            grid=(b,),
            in_specs=[pl.BlockSpec((1, n_heads, hd), lambda bi, pt: (bi, 0, 0)),
                      pl.BlockSpec((1, n_kv, hd), lambda bi, pt: (bi, 0, 0)),
                      pl.BlockSpec((1, n_kv, hd), lambda bi, pt: (bi, 0, 0)),
                      pl.BlockSpec((1, hd), lambda bi, pt: (0, 0)),
                      pl.BlockSpec((n_heads, 1), lambda bi, pt: (0, 0)),
                      hbm, hbm],
            out_specs=pl.BlockSpec((1, n_heads, hd), lambda bi, pt: (bi, 0, 0)),
            scratch_shapes=[pltpu.VMEM((2, ch, kvw), F32),
                            pltpu.SemaphoreType.DMA((2,)),
                            pltpu.VMEM((n_heads, n_pages * page), F32)],
        ),
        out_shape=jax.ShapeDtypeStruct((b, n_heads, hd), F32),
        compiler_params=_params(1),
        name="moba_sample",
    )(page_table, q3, kn3, vn3, gq.reshape(1, hd), slopes.reshape(n_heads, 1),
      cache_k.reshape(n_pool, page, kvw), cache_v.reshape(n_pool, page, kvw))


def _mem_attn_kernel(q_ref, mk_ref, mv_ref, g_ref, o_ref, *, nh, hd, hi):
    scale = hd ** -0.5
    tq = q_ref.shape[1]
    for h in range(nh):
        cs = slice(h * hd, (h + 1) * hd)
        qh = _rms(q_ref[0, :, cs], g_ref[...])
        if tq < 8:
            qh = jnp.broadcast_to(qh[0:1, :], (8, hd))
        mk = mk_ref[0, :, cs]
        mv = mv_ref[0, :, cs]
        if hi:
            s = _dot_hi(qh, mk, _NT) * scale
        else:
            s = _dot(qh.astype(BF16), mk.astype(BF16), _NT) * scale
        p = jnp.exp(s - jnp.max(s, axis=-1, keepdims=True))
        den = jnp.sum(p, axis=-1, keepdims=True)
        o = (_dot_hi(p, mv) if hi else _dot(p.astype(BF16), mv.astype(BF16))) / den
        o_ref[0, :, cs] = o[0:tq, :].astype(o_ref.dtype)


def _mem_attn(q3, col_off, mk, mv, g, *, hi, out_dtype):
    b, tq_all, _ = q3.shape
    n_mem, width = mk.shape[1], mk.shape[2]
    hd = g.shape[-1]
    tq = min(tq_all, 512)
    assert tq_all % tq == 0 and col_off % width == 0 and (tq_all == 1 or tq_all % 8 == 0)
    cb = col_off // width
    return pl.pallas_call(
        functools.partial(_mem_attn_kernel, nh=width // hd, hd=hd, hi=hi),
        grid=(b, tq_all // tq),
        in_specs=[pl.BlockSpec((1, tq, width), lambda bi, t: (bi, t, cb)),
                  pl.BlockSpec((1, n_mem, width), lambda bi, t: (bi, 0, 0)),
                  pl.BlockSpec((1, n_mem, width), lambda bi, t: (bi, 0, 0)),
                  pl.BlockSpec((1, hd), lambda bi, t: (0, 0))],
        out_specs=pl.BlockSpec((1, tq, width), lambda bi, t: (bi, t, 0)),
        out_shape=jax.ShapeDtypeStruct((b, tq_all, width), out_dtype),
        compiler_params=_params(2),
        name="mem_attn",
    )(q3, mk, mv, g.reshape(1, hd))


def _merge_kernel(a_ref, t_ref, m_ref, wc_ref, wa_ref, wm_ref, g0_ref, g1_ref, g2_ref, o_ref, *, hi):
    if hi:
        c = _dot_hi(a_ref[...], wc_ref[...])
        a = _dot_hi(t_ref[...], wa_ref[...])
        m = _dot_hi(m_ref[...], wm_ref[...])
    else:
        c = _dot(a_ref[...], wc_ref[...].astype(BF16))
        a = _dot(t_ref[...], wa_ref[...].astype(BF16))
        m = _dot(m_ref[...], wm_ref[...].astype(BF16))
    merged = (jax.nn.sigmoid(g0_ref[...]) * c + jax.nn.sigmoid(g1_ref[...]) * a
              + jax.nn.sigmoid(g2_ref[...]) * m)
    o_ref[...] = merged.astype(o_ref.dtype)


def _merge(act, attn, mem, w_conv_o, w_attn_o, w_mem_o, proj, o_g, *, hi, out_dtype, tn=512):
    m = act.shape[0]
    d = w_conv_o.shape[1]
    tm = min(m, 1024)
    assert m % tm == 0 and d % tn == 0 and o_g % tn == 0
    lhs = lambda a: pl.BlockSpec((tm, a.shape[1]), lambda i, j: (i, 0))
    wsp = lambda w: pl.BlockSpec((w.shape[0], tn), lambda i, j: (0, j))
    gsp = lambda br: pl.BlockSpec((tm, tn), lambda i, j: (i, (o_g + br * d) // tn + j))
    return pl.pallas_call(
        functools.partial(_merge_kernel, hi=hi),
        grid=(m // tm, d // tn),
        in_specs=[lhs(act), lhs(attn), lhs(mem), wsp(w_conv_o), wsp(w_attn_o), wsp(w_mem_o),
                  gsp(0), gsp(1), gsp(2)],
        out_specs=pl.BlockSpec((tm, tn), lambda i, j: (i, j)),
        out_shape=jax.ShapeDtypeStruct((m, d), out_dtype),
        compiler_params=_params(2),
        name="merge",
    )(act, attn, mem, w_conv_o, w_attn_o, w_mem_o, proj, proj, proj)


def _router_kernel(h_ref, g_ref, wr_ref, br_ref, hn_ref, idx_ref, gate_ref, *, n_exp):
    hn = _rms(h_ref[...], g_ref[...])
    hn_ref[...] = hn
    logits = _dot_hi(hn, wr_ref[...]) + br_ref[...]
    tm = logits.shape[0]
    neg = jnp.float32(-jnp.inf)
    lane = lax.broadcasted_iota(I32, (tm, n_exp), 1).astype(F32)
    out_lane = lax.broadcasted_iota(I32, (tm, LANE), 1)
    idx_out = jnp.zeros((tm, LANE), F32)
    val_out = jnp.full((tm, LANE), neg, F32)
    vals = logits
    for k in range(TOP_K):
        mx = jnp.max(vals, axis=-1, keepdims=True)
        am = jnp.min(jnp.where(vals == mx, lane, jnp.float32(n_exp)), axis=-1, keepdims=True)
        idx_out = jnp.where(out_lane == k, am, idx_out)
        val_out = jnp.where(out_lane == k, mx, val_out)
        vals = jnp.where(lane == am, neg, vals)
    e = jnp.exp(val_out - jnp.max(val_out, axis=-1, keepdims=True))
    gate_ref[...] = e / jnp.sum(e, axis=-1, keepdims=True)
    idx_ref[...] = idx_out.astype(I32)


def _router(h, g, w_router, b_router):
    m, d = h.shape
    n_exp = w_router.shape[1]
    tm = min(m, 256)
    assert m % tm == 0
    row = lambda w: pl.BlockSpec((tm, w), lambda i: (i, 0))
    return pl.pallas_call(
        functools.partial(_router_kernel, n_exp=n_exp),
        grid=(m // tm,),
        in_specs=[row(d), pl.BlockSpec((1, d), lambda i: (0, 0)),
                  pl.BlockSpec((d, n_exp), lambda i: (0, 0)),
                  pl.BlockSpec((1, n_exp), lambda i: (0, 0))],
        out_specs=[row(d), row(LANE), row(LANE)],
        out_shape=[jax.ShapeDtypeStruct((m, d), F32),
                   jax.ShapeDtypeStruct((m, LANE), I32),
                   jax.ShapeDtypeStruct((m, LANE), F32)],
        compiler_params=_params(1),
        name="router",
    )(h, g.reshape(1, d), w_router, b_router.reshape(1, n_exp))


def _moe_gather_kernel(meta_ref, blk_ref, valid_ref, src_ref, hp_ref, hs_ref, o_ref, rows, sem,
                       *, n_prompt):
    i = pl.program_id(0)
    valid = valid_ref[i]
    r_blk = rows.shape[0]

    @pl.when(i == 0)
    def _():
        rows[...] = jnp.zeros(rows.shape, F32)

    @pl.when(i < meta_ref[0])
    def _():
        def issue(r, carry):
            tok = src_ref[0, 0, r]

            @pl.when(tok < n_prompt)
            def _():
                pltpu.make_async_copy(hp_ref.at[pl.ds(tok, 1), :], rows.at[pl.ds(r, 1), :], sem).start()

            @pl.when(tok >= n_prompt)
            def _():
                pltpu.make_async_copy(hs_ref.at[pl.ds(tok - n_prompt, 1), :],
                                      rows.at[pl.ds(r, 1), :], sem).start()
            return carry

        lax.fori_loop(0, valid, issue, 0)

        def drain(r, carry):
            pltpu.make_async_copy(hp_ref.at[pl.ds(0, 1), :], rows.at[pl.ds(0, 1), :], sem).wait()
            return carry

        lax.fori_loop(0, valid, drain, 0)
        rid = lax.broadcasted_iota(I32, (r_blk, 1), 0)
        o_ref[...] = jnp.where(rid < valid, rows[...], 0.0).astype(o_ref.dtype)


def _moe_gather(meta, blk, valid, src, hn_p, hn_s, n_blocks):
    d = hn_p.shape[1]
    r_blk = MOE_BLOCK_ROWS
    hbm = pl.BlockSpec(memory_space=pl.ANY)
    return pl.pallas_call(
        functools.partial(_moe_gather_kernel, n_prompt=hn_p.shape[0]),
        grid_spec=pltpu.PrefetchScalarGridSpec(
            num_scalar_prefetch=3,
            grid=(n_blocks,),
            in_specs=[pl.BlockSpec((1, 1, r_blk), lambda i, mt, bk, vd: (i, 0, 0),
                                   memory_space=pltpu.SMEM),
                      hbm, hbm],
            out_specs=pl.BlockSpec((r_blk, d), lambda i, mt, bk, vd: (bk[i], 0)),
            scratch_shapes=[pltpu.VMEM((r_blk, d), F32), pltpu.SemaphoreType.DMA(())],
        ),
        out_shape=jax.ShapeDtypeStruct((n_blocks * r_blk, d), BF16),
        compiler_params=_params(1),
        name="moe_gather",
    )(meta, blk, valid, src, hn_p, hn_s)


def _moe_up_kernel(meta_ref, blk_ref, exp_ref, valid_ref, x_ref, wg_ref, wl_ref, bg_ref, bl_ref,
                   o_ref, wgb, wlb):
    i = pl.program_id(0)
    n_sub_all = x_ref.shape[0] // MOE_SUB

    @pl.when(i < meta_ref[0])
    def _():
        n_sub = (valid_ref[i] + MOE_SUB - 1) // MOE_SUB
        wgb[...] = wg_ref[0].astype(BF16)
        wlb[...] = wl_ref[0].astype(BF16)

        def body(s, carry):
            r0 = pl.multiple_of(s * MOE_SUB, MOE_SUB)
            x = x_ref[pl.ds(r0, MOE_SUB), :]
            glu = jnp.minimum(_dot(x, wgb[...]) + bg_ref[0], SWIGLU_LIMIT)
            lin = jnp.clip(_dot(x, wlb[...]) + bl_ref[0], -SWIGLU_LIMIT, SWIGLU_LIMIT)
            act = glu * jax.nn.sigmoid(SWIGLU_ALPHA * glu) * (lin + 1.0)
            o_ref[pl.ds(r0, MOE_SUB), :] = act.astype(o_ref.dtype)
            return carry

        lax.fori_loop(0, n_sub, body, 0)

        def zero(s, carry):
            r0 = pl.multiple_of(s * MOE_SUB, MOE_SUB)
            o_ref[pl.ds(r0, MOE_SUB), :] = jnp.zeros((MOE_SUB, o_ref.shape[1]), o_ref.dtype)
            return carry

        lax.fori_loop(n_sub, n_sub_all, zero, 0)


def _moe_up(meta, blk, exp, valid, xs, w_gu, b_gu, n_blocks, tf=256):
    n_exp, d, ff2 = w_gu.shape
    ff = ff2 // 2
    r_blk = MOE_BLOCK_ROWS
    nj = ff // tf
    assert ff % tf == 0

    def jj(i, j, mt):
        return jnp.where(i < mt[0], j, nj - 1)

    return pl.pallas_call(
        _moe_up_kernel,
        grid_spec=pltpu.PrefetchScalarGridSpec(
            num_scalar_prefetch=4,
            grid=(n_blocks, nj),
            in_specs=[pl.BlockSpec((r_blk, d), lambda i, j, mt, bk, ex, vd: (bk[i], 0)),
                      pl.BlockSpec((1, d, tf), lambda i, j, mt, bk, ex, vd: (ex[i], 0, jj(i, j, mt))),
                      pl.BlockSpec((1, d, tf), lambda i, j, mt, bk, ex, vd: (ex[i], 0, nj + jj(i, j, mt))),
                      pl.BlockSpec((1, 1, tf), lambda i, j, mt, bk, ex, vd: (ex[i], 0, jj(i, j, mt))),
                      pl.BlockSpec((1, 1, tf), lambda i, j, mt, bk, ex, vd: (ex[i], 0, nj + jj(i, j, mt)))],
            out_specs=pl.BlockSpec((r_blk, tf), lambda i, j, mt, bk, ex, vd: (bk[i], jj(i, j, mt))),
            scratch_shapes=[pltpu.VMEM((d, tf), BF16), pltpu.VMEM((d, tf), BF16)],
        ),
        out_shape=jax.ShapeDtypeStruct((n_blocks * r_blk, ff), BF16),
        compiler_params=_params(2),
        name="moe_up",
    )(meta, blk, exp, valid, xs, w_gu, w_gu, b_gu.reshape(n_exp, 1, ff2), b_gu.reshape(n_exp, 1, ff2))


def _moe_down_kernel(meta_ref, blk_ref, exp_ref, valid_ref, a_ref, w_ref, b_ref, o_ref, wb):
    i = pl.program_id(0)
    n_sub_all = a_ref.shape[0] // MOE_SUB

    @pl.when(i < meta_ref[0])
    def _():
        n_sub = (valid_ref[i] + MOE_SUB - 1) // MOE_SUB
        wb[...] = w_ref[0].astype(BF16)

        def body(s, carry):
            r0 = pl.multiple_of(s * MOE_SUB, MOE_SUB)
            o_ref[pl.ds(r0, MOE_SUB), :] = _dot(a_ref[pl.ds(r0, MOE_SUB), :], wb[...]) + b_ref[0]
            return carry

        lax.fori_loop(0, n_sub, body, 0)

        def zero(s, carry):
            r0 = pl.multiple_of(s * MOE_SUB, MOE_SUB)
            o_ref[pl.ds(r0, MOE_SUB), :] = jnp.zeros((MOE_SUB, o_ref.shape[1]), F32)
            return carry

        lax.fori_loop(n_sub, n_sub_all, zero, 0)


def _moe_down(meta, blk, exp, valid, act, w_down, b_down, n_blocks, tn=512):
    n_exp, ff, d = w_down.shape
    r_blk = MOE_BLOCK_ROWS
    nj = d // tn
    assert d % tn == 0

    def jj(i, j, mt):
        return jnp.where(i < mt[0], j, nj - 1)

    return pl.pallas_call(
        _moe_down_kernel,
        grid_spec=pltpu.PrefetchScalarGridSpec(
            num_scalar_prefetch=4,
            grid=(n_blocks, nj),
            in_specs=[pl.BlockSpec((r_blk, ff), lambda i, j, mt, bk, ex, vd: (bk[i], 0)),
                      pl.BlockSpec((1, ff, tn), lambda i, j, mt, bk, ex, vd: (ex[i], 0, jj(i, j, mt))),
                      pl.BlockSpec((1, 1, tn), lambda i, j, mt, bk, ex, vd: (ex[i], 0, jj(i, j, mt)))],
            out_specs=pl.BlockSpec((r_blk, tn), lambda i, j, mt, bk, ex, vd: (bk[i], jj(i, j, mt))),
            scratch_shapes=[pltpu.VMEM((ff, tn), BF16)],
        ),
        out_shape=jax.ShapeDtypeStruct((n_blocks * r_blk, d), F32),
        compiler_params=_params(2),
        name="moe_down",
    )(meta, blk, exp, valid, act, w_down, b_down.reshape(n_exp, 1, d))


def _moe_combine_kernel(dest_ref, h_ref, gate_ref, eo_ref, y_ref, rows, sem, *, tc):
    def issue(t, carry):
        for k in range(TOP_K):
            pltpu.make_async_copy(eo_ref.at[pl.ds(dest_ref[0, 0, t * TOP_K + k], 1), :],
                                  rows.at[k, pl.ds(t, 1), :], sem).start()
        return carry

    lax.fori_loop(0, tc, issue, 0)

    def drain(t, carry):
        for k in range(TOP_K):
            pltpu.make_async_copy(eo_ref.at[pl.ds(0, 1), :], rows.at[k, pl.ds(0, 1), :], sem).wait()
        return carry

    lax.fori_loop(0, tc, drain, 0)
    moe = jnp.zeros(h_ref.shape, F32)
    for k in range(TOP_K):
        moe = moe + gate_ref[:, k:k + 1] * rows[k]
    y_ref[...] = h_ref[...] + moe


def _moe_combine(dest, h, gate, expert_out):
    m, d = h.shape
    tc = min(m, 128)
    assert m % tc == 0
    return pl.pallas_call(
        functools.partial(_moe_combine_kernel, tc=tc),
        grid=(m // tc,),
        in_specs=[pl.BlockSpec((1, 1, tc * TOP_K), lambda i: (i, 0, 0), memory_space=pltpu.SMEM),
                  pl.BlockSpec((tc, d), lambda i: (i, 0)),
                  pl.BlockSpec((tc, LANE), lambda i: (i, 0)),
                  pl.BlockSpec(memory_space=pl.ANY)],
        out_specs=pl.BlockSpec((tc, d), lambda i: (i, 0)),
        out_shape=jax.ShapeDtypeStruct((m, d), F32),
        scratch_shapes=[pltpu.VMEM((TOP_K, tc, d), F32), pltpu.SemaphoreType.DMA(())],
        compiler_params=_params(1),
        name="moe_combine",
    )(dest.reshape(m // tc, 1, tc * TOP_K), h, gate, expert_out)


def _moe_tables(idx, n_exp, n_blocks):
    r_blk = MOE_BLOCK_ROWS
    e_flat = idx.reshape(-1)
    n_asg = e_flat.shape[0]
    onehot = (e_flat[:, None] == jnp.arange(n_exp, dtype=I32)[None, :]).astype(I32)
    csum = jnp.cumsum(onehot, axis=0)
    counts = csum[-1]
    pos = jnp.take_along_axis(csum, e_flat[:, None], axis=1)[:, 0] - 1
    nblk = (counts + r_blk - 1) // r_blk
    blk_end = jnp.cumsum(nblk)
    first = blk_end - nblk
    used = blk_end[-1]
    slot = (first[e_flat] + pos // r_blk) * r_blk + pos % r_blk
    src = jnp.zeros((n_blocks * r_blk,), I32).at[slot].set(jnp.arange(n_asg, dtype=I32) // TOP_K)
    bid = jnp.minimum(jnp.arange(n_blocks, dtype=I32), used - 1)
    exp = jnp.minimum(jnp.searchsorted(blk_end, bid, side='right'), n_exp - 1).astype(I32)
    valid = jnp.clip(counts[exp] - (bid - first[exp]) * r_blk, 0, r_blk).astype(I32)
    valid = jnp.where(jnp.arange(n_blocks) < used, valid, 0)
    meta = used.reshape(1).astype(I32)
    return meta, bid, exp, valid, src.reshape(n_blocks, 1, r_blk), slot.astype(I32)


def kernel(x_prompt, x_sample, cache_k, cache_v, cache_mem_k, cache_mem_v, state_conv, page_table,
           mem_prompt, norm_mix_g, w_in, q_norm_g, k_norm_g, w_attn_o, conv_dw_w, conv_dw_b,
           conv_ln_g, conv_ln_b, w_conv_o, mem_norm_g, w_mem_kv, mq_norm_g, mk_norm_g, w_mem_o,
           w_out, norm_ffn_g, w_router, b_router, w_gu, b_gu, w_down, b_down):
    depth = w_in.shape[0]
    assert depth == 1 and x_sample.shape[1] == 1
    bp, sp, d = x_prompt.shape
    bs = x_sample.shape[0]
    ch = conv_dw_w.shape[-1]
    hd = q_norm_g.shape[-1]
    n_kv = cache_k.shape[3]
    kvw = n_kv * hd
    qw = w_attn_o.shape[1]
    n_heads = qw // hd
    mqw = w_mem_o.shape[1]
    mem_heads, mhd = cache_mem_k.shape[3], cache_mem_k.shape[4]
    n_mem = cache_mem_k.shape[2]
    n_exp = w_router.shape[-1]
    o_ub, o_q = ch, 2 * ch
    o_k = o_q + qw
    o_v = o_k + kvw
    o_qm = o_v + kvw
    o_g = o_qm + mqw
    n_in = w_in.shape[-1]
    assert n_in == o_g + 3 * d and o_ub == ch
    slopes = jnp.asarray(2.0 ** (-8.0 * np.arange(1, n_heads + 1) / n_heads), dtype=F32)
    l = 0
    tp = bp * sp

    xp = x_prompt.reshape(tp, d)
    xn_p = _rmsnorm(xp, norm_mix_g[l], BF16)
    proj_p = _mm(xn_p, w_in[l])
    act_p, conv_p = _conv_prompt(proj_p, bp, sp, ch, conv_dw_w[l], conv_dw_b[l], conv_ln_g[l], conv_ln_b[l])
    kn_p, kmean_p = _headnorm(proj_p, o_k, kvw, k_norm_g[l], rows_per_block=MOBA_BLOCK, with_mean=True)
    attn_p = _moba_prompt(proj_p, kn_p, kmean_p.reshape(bp, sp // MOBA_BLOCK, kvw), q_norm_g[l], slopes,
                          bp, sp, o_q, o_v, n_heads, n_kv)
    memn = _rmsnorm(mem_prompt.reshape(bp * n_mem, d), mem_norm_g[l], BF16)
    mem_kv = _mm(memn, w_mem_kv[l])
    mk_p = _headnorm(mem_kv, 0, mqw, mk_norm_g[l], rows_per_block=256, with_mean=False)
    mv_p = mem_kv[:, mqw:]
    memo_p = _mem_attn(proj_p.reshape(bp, sp, n_in), o_qm, mk_p.reshape(bp, n_mem, mqw),
                       mv_p.reshape(bp, n_mem, mqw), mq_norm_g[l], hi=False, out_dtype=BF16)
    merged_p = _merge(act_p, attn_p, memo_p.reshape(tp, mqw), w_conv_o[l], w_attn_o[l], w_mem_o[l],
                      proj_p, o_g, hi=False, out_dtype=BF16)
    h_p = _mm(merged_p, w_out[l], residual=xp)
    hn_p, idx_p, gate_p = _router(h_p, norm_ffn_g[l], w_router[l], b_router[l])

    xs = x_sample.reshape(bs, d)
    xn_s = _rmsnorm(xs, norm_mix_g[l], F32)
    proj_s = _mm(xn_s, w_in[l], hi=True)
    proj_s3 = proj_s.reshape(bs, 1, n_in)
    act_s, conv_s = _conv_sample(proj_s3, state_conv[l], conv_dw_w[l], conv_dw_b[l], conv_ln_g[l], conv_ln_b[l])
    kn_s = _headnorm(proj_s, o_k, kvw, k_norm_g[l], rows_per_block=bs, with_mean=False)
    v_s = proj_s[:, o_v:o_v + kvw]
    attn_s = _moba_sample(page_table, proj_s[:, o_q:o_q + qw].reshape(bs, n_heads, hd),
                          kn_s.reshape(bs, n_kv, hd), v_s.reshape(bs, n_kv, hd), q_norm_g[l], slopes,
                          cache_k[l], cache_v[l])
    memo_s = _mem_attn(proj_s3, o_qm, cache_mem_k[l].reshape(bs, n_mem, mqw),
                       cache_mem_v[l].reshape(bs, n_mem, mqw), mq_norm_g[l], hi=True, out_dtype=F32)
    merged_s = _merge(act_s.reshape(bs, ch), attn_s.reshape(bs, qw), memo_s.reshape(bs, mqw),
                      w_conv_o[l], w_attn_o[l], w_mem_o[l], proj_s, o_g, hi=True, out_dtype=F32)
    h_s = _mm(merged_s, w_out[l], residual=xs, hi=True)
    hn_s, idx_s, gate_s = _router(h_s, norm_ffn_g[l], w_router[l], b_router[l])

    n_asg = (tp + bs) * TOP_K
    n_blocks = n_asg // MOE_BLOCK_ROWS + n_exp
    idx_all = jnp.concatenate([idx_p[:, :TOP_K], idx_s[:, :TOP_K]], axis=0)
    meta, bid, exp, valid, src, slot = _moe_tables(idx_all, n_exp, n_blocks)
    x_sorted = _moe_gather(meta, bid, valid, src, hn_p, hn_s, n_blocks)
    act_e = _moe_up(meta, bid, exp, valid, x_sorted, w_gu[l], b_gu[l], n_blocks)
    out_e = _moe_down(meta, bid, exp, valid, act_e, w_down[l], b_down[l], n_blocks)
    y_p = _moe_combine(slot[:tp * TOP_K], h_p, gate_p, out_e)
    y_s = _moe_combine(slot[tp * TOP_K:], h_s, gate_s, out_e)

    return (y_p.reshape(bp, sp, d), y_s.reshape(bs, 1, d),
            kn_p.reshape(depth, bp, sp, n_kv, hd),
            proj_p[:, o_v:o_v + kvw].reshape(depth, bp, sp, n_kv, hd),
            kn_s.reshape(depth, bs, 1, n_kv, hd), v_s.reshape(depth, bs, 1, n_kv, hd),
            mk_p.reshape(depth, bp, n_mem, mem_heads, mhd), mv_p.reshape(depth, bp, n_mem, mem_heads, mhd),
            conv_p.reshape(depth, bp, conv_p.shape[1], ch), conv_s.reshape(depth, bs, conv_s.shape[1], ch))
```

```python
import functools

import numpy as np
import jax
import jax.numpy as jnp
from jax import lax
from jax.experimental import pallas as pl
from jax.experimental.pallas import tpu as pltpu

F32 = jnp.float32
BF16 = jnp.bfloat16
I32 = jnp.int32

EPS = 1e-6
MOBA_BLOCK = 256
MOBA_TOP = 3
TOP_K = 4
SWIGLU_LIMIT = 7.0
SWIGLU_ALPHA = 1.702
CONV_HALO = 32

V7X_VMEM_LIMIT_BYTES = 56 * 1024 * 1024
LANE = 128

MOE_SUB = 256
MOE_BLOCK_ROWS = 5 * MOE_SUB
KV_CHUNK_PAGES = 8


def _params(n_axes):
    return pltpu.CompilerParams(dimension_semantics=("arbitrary",) * n_axes,
                                vmem_limit_bytes=V7X_VMEM_LIMIT_BYTES)


def _split_bf16(a):
    hi = a.astype(BF16)
    lo = (a - hi.astype(F32)).astype(BF16)
    return hi, lo


def _dot(a, b, dims=(((1,), (0,)), ((), ()))):
    return lax.dot_general(a, b, dims, preferred_element_type=F32)


def _dot_hi(a, b, dims=(((1,), (0,)), ((), ()))):
    ah, al = _split_bf16(a)
    bh, bl = _split_bf16(b)
    return _dot(ah, bh, dims) + (_dot(ah, bl, dims) + _dot(al, bh, dims))


_NT = (((1,), (1,)), ((), ()))


def _rms(x, g):
    return x * lax.rsqrt(jnp.mean(x * x, axis=-1, keepdims=True) + EPS) * g


def _rmsnorm_kernel(x_ref, g_ref, o_ref):
    o_ref[...] = _rms(x_ref[...], g_ref[...]).astype(o_ref.dtype)


def _rmsnorm(x, g, out_dtype):
    m, d = x.shape
    tm = min(m, 256)
    assert m % tm == 0
    return pl.pallas_call(
        _rmsnorm_kernel,
        grid=(m // tm,),
        in_specs=[pl.BlockSpec((tm, d), lambda i: (i, 0)),
                  pl.BlockSpec((1, d), lambda i: (0, 0))],
        out_specs=pl.BlockSpec((tm, d), lambda i: (i, 0)),
        out_shape=jax.ShapeDtypeStruct((m, d), out_dtype),
        compiler_params=_params(1),
        name="rmsnorm",
    )(x, g.reshape(1, d))


def _mm_kernel(x_ref, w_ref, *rest, hi, has_res):
    o_ref = rest[-1]
    if hi:
        acc = _dot_hi(x_ref[...], w_ref[...])
    else:
        acc = _dot(x_ref[...], w_ref[...].astype(BF16))
    if has_res:
        acc = rest[0][...] + acc
    o_ref[...] = acc


def _mm(x, w, *, residual=None, hi=False, tn=512):
    m, k = x.shape
    n = w.shape[1]
    tm = min(m, 1024)
    assert m % tm == 0 and n % tn == 0
    in_specs = [pl.BlockSpec((tm, k), lambda i, j: (i, 0)),
                pl.BlockSpec((k, tn), lambda i, j: (0, j))]
    args = [x, w]
    if residual is not None:
        in_specs.append(pl.BlockSpec((tm, tn), lambda i, j: (i, j)))
        args.append(residual)
    return pl.pallas_call(
        functools.partial(_mm_kernel, hi=hi, has_res=residual is not None),
        grid=(m // tm, n // tn),
        in_specs=in_specs,
        out_specs=pl.BlockSpec((tm, tn), lambda i, j: (i, j)),
        out_shape=jax.ShapeDtypeStruct((m, n), F32),
        compiler_params=_params(2),
        name="matmul",
    )(*args)


def _headnorm_kernel(x_ref, g_ref, o_ref, *mean_ref, hd, nh):
    for h in range(nh):
        y = _rms(x_ref[:, h * hd:(h + 1) * hd], g_ref[...])
        o_ref[:, h * hd:(h + 1) * hd] = y
        if mean_ref:
            mean_ref[0][0, :, h * hd:(h + 1) * hd] = jnp.mean(y, axis=0, keepdims=True)


def _headnorm(x, col_off, width, g, *, rows_per_block, with_mean):
    m = x.shape[0]
    hd = g.shape[-1]
    tm = min(m, rows_per_block)
    assert m % tm == 0 and col_off % width == 0 and width % hd == 0
    cb = col_off // width
    out_shape = [jax.ShapeDtypeStruct((m, width), F32)]
    out_specs = [pl.BlockSpec((tm, width), lambda i: (i, 0))]
    if with_mean:
        out_shape.append(jax.ShapeDtypeStruct((m // tm, 1, width), F32))
        out_specs.append(pl.BlockSpec((1, 1, width), lambda i: (i, 0, 0)))
    res = pl.pallas_call(
        functools.partial(_headnorm_kernel, hd=hd, nh=width // hd),
        grid=(m // tm,),
        in_specs=[pl.BlockSpec((tm, width), lambda i: (i, cb)),
                  pl.BlockSpec((1, hd), lambda i: (0, 0))],
        out_specs=out_specs,
        out_shape=out_shape,
        compiler_params=_params(1),
        name="headnorm",
    )(x, g.reshape(1, hd))
    return res if with_mean else res[0]


def _conv_prompt_kernel(ua_ref, ub_ref, w_ref, b_ref, g_ref, beta_ref, act_ref, new_ref,
                        full_ref, conv_ref, *, tt, nt, width, ch):
    t = pl.program_id(1)
    hist = width - 1
    lead = CONV_HALO - hist

    @pl.when(t == 0)
    def _():
        full_ref[0:CONV_HALO, :] = jnp.zeros((CONV_HALO, ch), F32)

    full_ref[CONV_HALO:CONV_HALO + tt, :] = ua_ref[...] * jax.nn.sigmoid(ub_ref[...])

    rc, cc = 32, 256
    for r in range(tt // rc):
        for c in range(ch // cc):
            cs = slice(c * cc, (c + 1) * cc)
            acc = jnp.zeros((rc, cc), F32)
            for w in range(width):
                r0 = lead + w + r * rc
                acc = acc + full_ref[r0:r0 + rc, cs] * w_ref[w:w + 1, cs]
            conv_ref[r * rc:(r + 1) * rc, cs] = acc + b_ref[:, cs]

    for r in range(tt // rc):
        cf = conv_ref[r * rc:(r + 1) * rc, :]
        mu = jnp.mean(cf, axis=-1, keepdims=True)
        d = cf - mu
        var = jnp.mean(d * d, axis=-1, keepdims=True)
        cn = d * lax.rsqrt(var + EPS) * g_ref[...] + beta_ref[...]
        act_ref[r * rc:(r + 1) * rc, :] = (cn * jax.nn.sigmoid(cn)).astype(act_ref.dtype)

    tail = full_ref[tt + lead:tt + CONV_HALO, :]
    full_ref[lead:CONV_HALO, :] = tail

    @pl.when(t == nt - 1)
    def _():
        new_ref[0] = tail


def _conv_prompt(proj, b, s, ch, dw_w, dw_b, ln_g, ln_b):
    width = dw_w.shape[0]
    tt = 256
    assert s % tt == 0 and ch % 256 == 0 and width - 1 <= CONV_HALO
    nt = s // tt
    row = lambda bi, ti: (bi * nt + ti, 0)
    vec = pl.BlockSpec((1, ch), lambda bi, ti: (0, 0))
    return pl.pallas_call(
        functools.partial(_conv_prompt_kernel, tt=tt, nt=nt, width=width, ch=ch),
        grid=(b, nt),
        in_specs=[pl.BlockSpec((tt, ch), row),
                  pl.BlockSpec((tt, ch), lambda bi, ti: (bi * nt + ti, 1)),
                  pl.BlockSpec((width, ch), lambda bi, ti: (0, 0)),
                  vec, vec, vec],
        out_specs=[pl.BlockSpec((tt, ch), row),
                   pl.BlockSpec((1, width - 1, ch), lambda bi, ti: (bi, 0, 0))],
        out_shape=[jax.ShapeDtypeStruct((b * s, ch), BF16),
                   jax.ShapeDtypeStruct((b, width - 1, ch), F32)],
        scratch_shapes=[pltpu.VMEM((CONV_HALO + tt, ch), F32),
                        pltpu.VMEM((tt, ch), F32)],
        compiler_params=_params(2),
        name="conv_prompt",
    )(proj, proj, dw_w, dw_b.reshape(1, ch), ln_g.reshape(1, ch), ln_b.reshape(1, ch))


def _conv_sample_kernel(ua_ref, ub_ref, hist_ref, w_ref, b_ref, g_ref, beta_ref,
                        act_ref, new_ref, *, width):
    hist = width - 1
    u = ua_ref[0] * jax.nn.sigmoid(ub_ref[0])
    h = hist_ref[0]
    cf = (jnp.sum(h * w_ref[0:hist, :], axis=0, keepdims=True)
          + u * w_ref[hist:width, :] + b_ref[...])
    mu = jnp.mean(cf, axis=-1, keepdims=True)
    d = cf - mu
    var = jnp.mean(d * d, axis=-1, keepdims=True)
    cn = d * lax.rsqrt(var + EPS) * g_ref[...] + beta_ref[...]
    act_ref[0] = cn * jax.nn.sigmoid(cn)
    new_ref[0, 0:hist - 1, :] = hist_ref[0, 1:hist, :]
    new_ref[0, hist - 1:hist, :] = u


def _conv_sample(proj3, state, dw_w, dw_b, ln_g, ln_b):
    b, hist, ch = state.shape
    width = dw_w.shape[0]
    vec = pl.BlockSpec((1, ch), lambda bi: (0, 0))
    return pl.pallas_call(
        functools.partial(_conv_sample_kernel, width=width),
        grid=(b,),
        in_specs=[pl.BlockSpec((1, 1, ch), lambda bi: (bi, 0, 0)),
                  pl.BlockSpec((1, 1, ch), lambda bi: (bi, 0, 1)),
                  pl.BlockSpec((1, hist, ch), lambda bi: (bi, 0, 0)),
                  pl.BlockSpec((width, ch), lambda bi: (0, 0)),
                  vec, vec, vec],
        out_specs=[pl.BlockSpec((1, 1, ch), lambda bi: (bi, 0, 0)),
                   pl.BlockSpec((1, hist, ch), lambda bi: (bi, 0, 0))],
        out_shape=[jax.ShapeDtypeStruct((b, 1, ch), F32),
                   jax.ShapeDtypeStruct((b, hist, ch), F32)],
        compiler_params=_params(1),
        name="conv_sample",
    )(proj3, proj3, state, dw_w, dw_b.reshape(1, ch), ln_g.reshape(1, ch), ln_b.reshape(1, ch))


def _rank_select(gate, lane, n_valid, n_top, n_cols):
    neg = jnp.float32(-jnp.inf)
    gm = jnp.where(lane < n_valid, gate, neg)
    rank = jnp.zeros(gate.shape, F32)
    for m in range(n_cols):
        gcol = gm[:, m:m + 1]
        beats = (gcol > gm) | ((gcol == gm) & (m < lane))
        rank = rank + jnp.where(beats, 1.0, 0.0)
    return jnp.where((rank < n_top) & (lane < n_valid), 1.0, 0.0)


def _rank_select_rows(gate, n_valid, n_top):
    neg = jnp.float32(-jnp.inf)
    rid = lax.broadcasted_iota(I32, gate.shape, 0)
    gm = jnp.where(rid < n_valid, gate, neg)
    rank = jnp.zeros(gate.shape, F32)
    for m in range(gate.shape[0]):
        grow = gm[m:m + 1, :]
        beats = (grow > gm) | ((grow == gm) & (m < rid))
        rank = rank + jnp.where(beats, 1.0, 0.0)
    return jnp.where((rank < n_top) & (rid < n_valid), 1.0, 0.0)


def _moba_prompt_kernel(slopes_ref, q_ref, k_ref, v_ref, km_ref, gq_ref, o_ref, kb, vb,
                        *, nb, blk, group, hd):
    kvh = pl.program_id(1)
    i = pl.program_id(2)
    scale = hd ** -0.5
    neg = jnp.float32(-jnp.inf)

    @pl.when(i == 0)
    def _():
        kb[...] = k_ref[...].astype(BF16)
        vb[...] = v_ref[...].astype(BF16)

    km = km_ref[0]
    row = lax.broadcasted_iota(I32, (blk, blk), 0)
    col = lax.broadcasted_iota(I32, (blk, blk), 1)
    tri = col <= row
    rc = (row - col).astype(F32)
    lane = lax.broadcasted_iota(I32, (blk, LANE), 1)
    i0 = pl.multiple_of(i * blk, blk)
    k_own = kb[pl.ds(i0, blk), :]
    v_own = vb[pl.ds(i0, blk), :]
    for hh in range(group):
        qh = _rms(q_ref[:, hh * hd:(hh + 1) * hd], gq_ref[...])
        sel_t = _rank_select_rows(_dot_hi(km, qh, _NT), i, MOBA_TOP)
        sel = jnp.transpose(jnp.concatenate([sel_t, jnp.zeros((LANE - nb, blk), F32)], axis=0))
        slope = slopes_ref[kvh * group + hh]
        qb = qh.astype(BF16)
        s = jnp.where(tri, _dot(qb, k_own, _NT) * scale - slope * rc, neg)
        m0 = jnp.max(s, axis=-1, keepdims=True)
        p = jnp.exp(s - m0)
        l0 = jnp.sum(p, axis=-1, keepdims=True)
        acc0 = _dot(p.astype(BF16), v_own)

        def past_block(j, carry):
            m, l, acc = carry
            j0 = pl.multiple_of(j * blk, blk)
            picked = jnp.sum(jnp.where(lane == j, sel, 0.0), axis=-1, keepdims=True)
            dist = rc + ((i - j) * blk).astype(F32)
            sj = _dot(qb, kb[pl.ds(j0, blk), :], _NT) * scale - slope * dist
            sj = jnp.where(jnp.broadcast_to(picked, (blk, blk)) > 0.5, sj, neg)
            m_new = jnp.maximum(m, jnp.max(sj, axis=-1, keepdims=True))
            alpha = jnp.exp(m - m_new)
            pj = jnp.exp(sj - m_new)
            l_new = alpha * l + jnp.sum(pj, axis=-1, keepdims=True)
            acc_new = alpha * acc + _dot(pj.astype(BF16), vb[pl.ds(j0, blk), :])
            return m_new, l_new, acc_new

        _, l, acc = lax.fori_loop(0, i, past_block, (m0, l0, acc0))
        o_ref[:, hh * hd:(hh + 1) * hd] = (acc / l).astype(o_ref.dtype)


def _moba_prompt(proj, kn, kmean, gq, slopes, b, s, o_q, o_v, n_heads, n_kv):
    hd = gq.shape[-1]
    blk = MOBA_BLOCK
    nb = s // blk
    group = n_heads // n_kv
    gw = group * hd
    assert s % blk == 0 and o_q % gw == 0 and o_v % hd == 0 and nb <= LANE
    return pl.pallas_call(
        functools.partial(_moba_prompt_kernel, nb=nb, blk=blk, group=group, hd=hd),
        grid_spec=pltpu.PrefetchScalarGridSpec(
            num_scalar_prefetch=1,
            grid=(b, n_kv, nb),
            in_specs=[pl.BlockSpec((blk, gw), lambda bi, g, i, sl: (bi * nb + i, o_q // gw + g)),
                      pl.BlockSpec((s, hd), lambda bi, g, i, sl: (bi, g)),
                      pl.BlockSpec((s, hd), lambda bi, g, i, sl: (bi, o_v // hd + g)),
                      pl.BlockSpec((1, nb, hd), lambda bi, g, i, sl: (bi, 0, g)),
                      pl.BlockSpec((1, hd), lambda bi, g, i, sl: (0, 0))],
            out_specs=pl.BlockSpec((blk, gw), lambda bi, g, i, sl: (bi * nb + i, g)),
            scratch_shapes=[pltpu.VMEM((s, hd), BF16), pltpu.VMEM((s, hd), BF16)],
        ),
        out_shape=jax.ShapeDtypeStruct((b * s, n_heads * hd), BF16),
        compiler_params=_params(3),
        name="moba_prompt",
    )(slopes, proj, kn, proj, kmean, gq.reshape(1, hd))


def _moba_sample_kernel(pt_ref, q_ref, kn_ref, vn_ref, gq_ref, slope_ref, ck_ref, cv_ref, o_ref,
                        buf, sem, s_ref, *, n_pages, page, n_kv, hd, blk):
    b = pl.program_id(0)
    n_heads = q_ref.shape[1]
    group = n_heads // n_kv
    ppc = KV_CHUNK_PAGES
    rpp = page * n_kv
    ch = ppc * rpp
    kpc = ppc * page
    n_chunks = n_pages // ppc
    past = n_pages * page
    n_blk = past // blk
    bpc = kpc // blk
    cpb = blk * n_kv
    kv_shift = n_kv.bit_length() - 1
    scale = hd ** -0.5
    neg = jnp.float32(-jnp.inf)

    def copies(step):
        src = ck_ref if step < n_chunks else cv_ref
        c = step % n_chunks
        slot = step % 2
        return [pltpu.make_async_copy(src.at[pt_ref[b, c * ppc + p]],
                                      buf.at[slot, pl.ds(p * rpp, rpp), :],
                                      sem.at[slot]) for p in range(ppc)]

    def start(step):
        for cp in copies(step):
            cp.start()

    def wait(step):
        for cp in copies(step):
            cp.wait()

    qn = _rms(q_ref[0], gq_ref[...])
    head_grp = lax.broadcasted_iota(I32, (n_heads, 1), 0) // group
    lane_b = lax.broadcasted_iota(I32, (n_heads, LANE), 1)
    gates = jnp.zeros((n_heads, LANE), F32)
    col = lax.broadcasted_iota(I32, (n_heads, ch), 1)
    mine = jnp.bitwise_and(col, n_kv - 1) == head_grp
    key = jnp.right_shift(col, kv_shift)

    start(0)
    for c in range(n_chunks):
        start(c + 1)
        wait(c)
        sc = _dot_hi(qn, buf[c % 2], _NT)
        s_ref[:, c * ch:(c + 1) * ch] = sc
        own_sc = jnp.where(mine, sc, 0.0)
        for k in range(bpc):
            gsum = jnp.sum(own_sc[:, k * cpb:(k + 1) * cpb], axis=-1, keepdims=True) * (1.0 / blk)
            gates = jnp.where(lane_b == c * bpc + k, gsum, gates)

    sel = _rank_select(gates, lane_b, n_blk, min(MOBA_TOP, n_blk + 1), n_blk)
    self_all = _dot_hi(qn, kn_ref[0], _NT)
    grp_lane = lax.broadcasted_iota(I32, (n_heads, n_kv), 1)
    s_self = jnp.sum(jnp.where(grp_lane == head_grp, self_all, 0.0), axis=-1, keepdims=True) * scale
    slope = slope_ref[...]
    mx = s_self
    for c in range(n_chunks):
        dist = (past - c * kpc - key).astype(F32)
        sc = s_ref[:, c * ch:(c + 1) * ch] * scale - slope * dist
        keep = jnp.concatenate(
            [jnp.broadcast_to(
                jnp.sum(jnp.where(lane_b == c * bpc + k, sel, 0.0), axis=-1, keepdims=True),
                (n_heads, cpb)) for k in range(bpc)], axis=-1)
        sc = jnp.where((keep > 0.5) & mine, sc, neg)
        s_ref[:, c * ch:(c + 1) * ch] = sc
        mx = jnp.maximum(mx, jnp.max(sc, axis=-1, keepdims=True))

    p_self = jnp.exp(s_self - mx)
    den = p_self
    acc = jnp.zeros((n_heads, hd), F32)
    for c in range(n_chunks):
        step = n_chunks + c
        if step + 1 < 2 * n_chunks:
            start(step + 1)
        wait(step)
        p = jnp.exp(s_ref[:, c * ch:(c + 1) * ch] - mx)
        den = den + jnp.sum(p, axis=-1, keepdims=True)
        acc = acc + _dot_hi(p, buf[step % 2])
    v_self = jnp.zeros((n_heads, hd), F32)
    for g in range(n_kv):
        v_self = jnp.where(head_grp == g, vn_ref[0, g:g + 1, :], v_self)
    o_ref[0] = (acc + p_self * v_self) / den


def _moba_sample(page_table, q3, kn3, vn3, gq, slopes, cache_k, cache_v):
    b, n_heads, hd = q3.shape
    n_kv = kn3.shape[1]
    n_pages = page_table.shape[1]
    n_pool, page = cache_k.shape[0], cache_k.shape[1]
    assert (n_pages * page) % MOBA_BLOCK == 0 and n_pages % KV_CHUNK_PAGES == 0
    assert (KV_CHUNK_PAGES * page) % MOBA_BLOCK == 0 and (n_pages * page) // MOBA_BLOCK <= LANE
    assert n_kv & (n_kv - 1) == 0
    ch = KV_CHUNK_PAGES * page * n_kv
    hbm = pl.BlockSpec(memory_space=pl.ANY)
    return pl.pallas_call(
        functools.partial(_moba_sample_kernel, n_pages=n_pages, page=page, n_kv=n_kv, hd=hd,
                          blk=MOBA_BLOCK),
        grid_spec=pltpu.PrefetchScalarGridSpec(
            num_scalar_prefetch=1,
            grid=(b,),
            in_specs=[pl.BlockSpec((1, n_heads, hd), lambda bi, pt: (bi, 0, 0)),
                      pl.BlockSpec((1, n_kv, hd), lambda bi, pt: (bi, 0, 0)),
                      pl.BlockSpec((1, n_kv, hd), lambda bi, pt: (bi, 0, 0)),
                      pl.BlockSpec((1, hd), lambda bi, pt: (0, 0)),
                      pl.BlockSpec((n_heads, 1), lambda bi, pt: (0, 0)),
                      hbm, hbm],
            out_specs=pl.BlockSpec((1, n_heads, hd), lambda bi, pt: (bi, 0, 0)),
            scratch_shapes=[pltpu.VMEM((2, ch, hd), F32),
                            pltpu.SemaphoreType.DMA((2,)),
                            pltpu.VMEM((n_heads, n_pages * page * n_kv), F32)],
        ),
        out_shape=jax.ShapeDtypeStruct((b, n_heads, hd), F32),
        compiler_params=_params(1),
        name="moba_sample",
    )(page_table, q3, kn3, vn3, gq.reshape(1, hd), slopes.reshape(n_heads, 1),
      cache_k.reshape(n_pool, page * n_kv, hd), cache_v.reshape(n_pool, page * n_kv, hd))


def _mem_attn_kernel(q_ref, mk_ref, mv_ref, g_ref, o_ref, *, nh, hd, hi):
    scale = hd ** -0.5
    tq = q_ref.shape[1]
    for h in range(nh):
        cs = slice(h * hd, (h + 1) * hd)
        qh = _rms(q_ref[0, :, cs], g_ref[...])
        if tq < 8:
            qh = jnp.broadcast_to(qh[0:1, :], (8, hd))
        mk = mk_ref[0, :, cs]
        mv = mv_ref[0, :, cs]
        if hi:
            s = _dot_hi(qh, mk, _NT) * scale
        else:
            s = _dot(qh.astype(BF16), mk.astype(BF16), _NT) * scale
        p = jnp.exp(s - jnp.max(s, axis=-1, keepdims=True))
        den = jnp.sum(p, axis=-1, keepdims=True)
        o = (_dot_hi(p, mv) if hi else _dot(p.astype(BF16), mv.astype(BF16))) / den
        o_ref[0, :, cs] = o[0:tq, :].astype(o_ref.dtype)


def _mem_attn(q3, col_off, mk, mv, g, *, hi, out_dtype):
    b, tq_all, _ = q3.shape
    n_mem, width = mk.shape[1], mk.shape[2]
    hd = g.shape[-1]
    tq = min(tq_all, 512)
    assert tq_all % tq == 0 and col_off % width == 0 and (tq_all == 1 or tq_all % 8 == 0)
    cb = col_off // width
    return pl.pallas_call(
        functools.partial(_mem_attn_kernel, nh=width // hd, hd=hd, hi=hi),
        grid=(b, tq_all // tq),
        in_specs=[pl.BlockSpec((1, tq, width), lambda bi, t: (bi, t, cb)),
                  pl.BlockSpec((1, n_mem, width), lambda bi, t: (bi, 0, 0)),
                  pl.BlockSpec((1, n_mem, width), lambda bi, t: (bi, 0, 0)),
                  pl.BlockSpec((1, hd), lambda bi, t: (0, 0))],
        out_specs=pl.BlockSpec((1, tq, width), lambda bi, t: (bi, t, 0)),
        out_shape=jax.ShapeDtypeStruct((b, tq_all, width), out_dtype),
        compiler_params=_params(2),
        name="mem_attn",
    )(q3, mk, mv, g.reshape(1, hd))


def _merge_kernel(a_ref, t_ref, m_ref, wc_ref, wa_ref, wm_ref, g0_ref, g1_ref, g2_ref, o_ref, *, hi):
    if hi:
        c = _dot_hi(a_ref[...], wc_ref[...])
        a = _dot_hi(t_ref[...], wa_ref[...])
        m = _dot_hi(m_ref[...], wm_ref[...])
    else:
        c = _dot(a_ref[...], wc_ref[...].astype(BF16))
        a = _dot(t_ref[...], wa_ref[...].astype(BF16))
        m = _dot(m_ref[...], wm_ref[...].astype(BF16))
    merged = (jax.nn.sigmoid(g0_ref[...]) * c + jax.nn.sigmoid(g1_ref[...]) * a
              + jax.nn.sigmoid(g2_ref[...]) * m)
    o_ref[...] = merged.astype(o_ref.dtype)


def _merge(act, attn, mem, w_conv_o, w_attn_o, w_mem_o, proj, o_g, *, hi, out_dtype, tn=512):
    m = act.shape[0]
    d = w_conv_o.shape[1]
    tm = min(m, 1024)
    assert m % tm == 0 and d % tn == 0 and o_g % tn == 0
    lhs = lambda a: pl.BlockSpec((tm, a.shape[1]), lambda i, j: (i, 0))
    wsp = lambda w: pl.BlockSpec((w.shape[0], tn), lambda i, j: (0, j))
    gsp = lambda br: pl.BlockSpec((tm, tn), lambda i, j: (i, (o_g + br * d) // tn + j))
    return pl.pallas_call(
        functools.partial(_merge_kernel, hi=hi),
        grid=(m // tm, d // tn),
        in_specs=[lhs(act), lhs(attn), lhs(mem), wsp(w_conv_o), wsp(w_attn_o), wsp(w_mem_o),
                  gsp(0), gsp(1), gsp(2)],
        out_specs=pl.BlockSpec((tm, tn), lambda i, j: (i, j)),
        out_shape=jax.ShapeDtypeStruct((m, d), out_dtype),
        compiler_params=_params(2),
        name="merge",
    )(act, attn, mem, w_conv_o, w_attn_o, w_mem_o, proj, proj, proj)


def _router_kernel(h_ref, g_ref, wr_ref, br_ref, *rest, n_exp):
    hn_ref, idx_ref, gate_ref = rest[-3:]
    hn = _rms(h_ref[...], g_ref[...])
    hn_ref[...] = hn
    logits = _dot_hi(hn, wr_ref[...]) + br_ref[...]
    tm = logits.shape[0]
    neg = jnp.float32(-jnp.inf)
    lane = lax.broadcasted_iota(I32, (tm, n_exp), 1).astype(F32)
    out_lane = lax.broadcasted_iota(I32, (tm, LANE), 1)
    idx_out = jnp.zeros((tm, LANE), F32)
    val_out = jnp.full((tm, LANE), neg, F32)
    vals = logits
    for k in range(TOP_K):
        mx = jnp.max(vals, axis=-1, keepdims=True)
        am = jnp.min(jnp.where(vals == mx, lane, jnp.float32(n_exp)), axis=-1, keepdims=True)
        idx_out = jnp.where(out_lane == k, am, idx_out)
        val_out = jnp.where(out_lane == k, mx, val_out)
        vals = jnp.where(lane == am, neg, vals)
    e = jnp.exp(val_out - jnp.max(val_out, axis=-1, keepdims=True))
    gate_ref[...] = e / jnp.sum(e, axis=-1, keepdims=True)
    idx_ref[...] = idx_out.astype(I32)


def _router(h, g, w_router, b_router, *, total_rows, row_off=0, hn_into=None):
    m, d = h.shape
    n_exp = w_router.shape[1]
    tm = min(m, 256)
    assert m % tm == 0 and row_off % tm == 0
    off = row_off // tm
    row = lambda w: pl.BlockSpec((tm, w), lambda i: (i, 0))
    in_specs = [row(d), pl.BlockSpec((1, d), lambda i: (0, 0)),
                pl.BlockSpec((d, n_exp), lambda i: (0, 0)),
                pl.BlockSpec((1, n_exp), lambda i: (0, 0))]
    args = [h, g.reshape(1, d), w_router, b_router.reshape(1, n_exp)]
    aliases = {}
    if hn_into is not None:
        in_specs.append(pl.BlockSpec(memory_space=pl.ANY))
        args.append(hn_into)
        aliases = {len(args) - 1: 0}
    return pl.pallas_call(
        functools.partial(_router_kernel, n_exp=n_exp),
        grid=(m // tm,),
        in_specs=in_specs,
        out_specs=[pl.BlockSpec((tm, d), lambda i: (i + off, 0)), row(LANE), row(LANE)],
        out_shape=[jax.ShapeDtypeStruct((total_rows, d), F32),
                   jax.ShapeDtypeStruct((m, LANE), I32),
                   jax.ShapeDtypeStruct((m, LANE), F32)],
        input_output_aliases=aliases,
        compiler_params=_params(1),
        name="router",
    )(*args)


def _moe_gather_kernel(meta_ref, blk_ref, valid_ref, src_ref, hn_ref, o_ref, rows, sem):
    i = pl.program_id(0)
    valid = valid_ref[i]
    r_blk = rows.shape[0]
    grp = 8

    @pl.when(i == 0)
    def _():
        rows[...] = jnp.zeros(rows.shape, F32)

    def row_copy(tok, r):
        return pltpu.make_async_copy(hn_ref.at[pl.ds(tok, 1), :], rows.at[pl.ds(r, 1), :], sem)

    @pl.when(i < meta_ref[0])
    def _():
        n_grp = (valid + grp - 1) // grp

        def issue(g, carry):
            for u in range(grp):
                row_copy(src_ref[0, 0, g * grp + u], g * grp + u).start()
            return carry

        lax.fori_loop(0, n_grp, issue, 0)

        def drain(g, carry):
            for u in range(grp):
                row_copy(0, 0).wait()
            return carry

        lax.fori_loop(0, n_grp, drain, 0)
        rid = lax.broadcasted_iota(I32, (r_blk, 1), 0)
        o_ref[...] = jnp.where(rid < valid, rows[...], 0.0).astype(o_ref.dtype)


def _moe_gather(meta, blk, valid, src, hn, n_blocks):
    d = hn.shape[1]
    r_blk = MOE_BLOCK_ROWS
    assert r_blk % 8 == 0
    hbm = pl.BlockSpec(memory_space=pl.ANY)
    return pl.pallas_call(
        _moe_gather_kernel,
        grid_spec=pltpu.PrefetchScalarGridSpec(
            num_scalar_prefetch=3,
            grid=(n_blocks,),
            in_specs=[pl.BlockSpec((1, 1, r_blk), lambda i, mt, bk, vd: (i, 0, 0),
                                   memory_space=pltpu.SMEM),
                      hbm],
            out_specs=pl.BlockSpec((r_blk, d), lambda i, mt, bk, vd: (bk[i], 0)),
            scratch_shapes=[pltpu.VMEM((r_blk, d), F32), pltpu.SemaphoreType.DMA(())],
        ),
        out_shape=jax.ShapeDtypeStruct((n_blocks * r_blk, d), BF16),
        compiler_params=_params(1),
        name="moe_gather",
    )(meta, blk, valid, src, hn)


def _moe_block_rows(valid, r_blk, cast_weights, rows, o_ref):
    half = MOE_SUB // 2
    n_full = valid // MOE_SUB
    rem = valid - n_full * MOE_SUB
    for k in range(r_blk // MOE_SUB + 1):
        @pl.when(n_full == k)
        def _(k=k):
            cast_weights()
            for s in range(k):
                rows(s * MOE_SUB, MOE_SUB)

    r_tail = pl.multiple_of(n_full * MOE_SUB, MOE_SUB)

    @pl.when(rem > half)
    def _():
        rows(r_tail, MOE_SUB)

    @pl.when((rem > 0) & (rem <= half))
    def _():
        rows(r_tail, half)

    def zero(s, carry):
        r0 = pl.multiple_of(s * half, half)
        o_ref[pl.ds(r0, half), :] = jnp.zeros((half, o_ref.shape[1]), o_ref.dtype)
        return carry

    lax.fori_loop((valid + half - 1) // half, r_blk // half, zero, 0)


def _moe_up_kernel(meta_ref, blk_ref, exp_ref, valid_ref, x_ref, wg_ref, wl_ref, bg_ref, bl_ref,
                   o_ref, wgb, wlb):
    i = pl.program_id(0)

    def rows(r0, n):
        x = x_ref[pl.ds(r0, n), :]
        glu = jnp.minimum(_dot(x, wgb[...]) + bg_ref[0], SWIGLU_LIMIT)
        lin = jnp.clip(_dot(x, wlb[...]) + bl_ref[0], -SWIGLU_LIMIT, SWIGLU_LIMIT)
        act = glu * jax.nn.sigmoid(SWIGLU_ALPHA * glu) * (lin + 1.0)
        o_ref[pl.ds(r0, n), :] = act.astype(o_ref.dtype)

    def cast_weights():
        wgb[...] = wg_ref[0].astype(BF16)
        wlb[...] = wl_ref[0].astype(BF16)

    @pl.when(i < meta_ref[0])
    def _():
        _moe_block_rows(valid_ref[i], x_ref.shape[0], cast_weights, rows, o_ref)


def _moe_up(meta, blk, exp, valid, xs, w_gu, b_gu, n_blocks, tf=256):
    n_exp, d, ff2 = w_gu.shape
    ff = ff2 // 2
    r_blk = MOE_BLOCK_ROWS
    nj = ff // tf
    assert ff % tf == 0

    def jj(i, j, mt):
        return jnp.where(i < mt[0], j, nj - 1)

    return pl.pallas_call(
        _moe_up_kernel,
        grid_spec=pltpu.PrefetchScalarGridSpec(
            num_scalar_prefetch=4,
            grid=(n_blocks, nj),
            in_specs=[pl.BlockSpec((r_blk, d), lambda i, j, mt, bk, ex, vd: (bk[i], 0)),
                      pl.BlockSpec((1, d, tf), lambda i, j, mt, bk, ex, vd: (ex[i], 0, jj(i, j, mt))),
                      pl.BlockSpec((1, d, tf), lambda i, j, mt, bk, ex, vd: (ex[i], 0, nj + jj(i, j, mt))),
                      pl.BlockSpec((1, 1, tf), lambda i, j, mt, bk, ex, vd: (ex[i], 0, jj(i, j, mt))),
                      pl.BlockSpec((1, 1, tf), lambda i, j, mt, bk, ex, vd: (ex[i], 0, nj + jj(i, j, mt)))],
            out_specs=pl.BlockSpec((r_blk, tf), lambda i, j, mt, bk, ex, vd: (bk[i], jj(i, j, mt))),
            scratch_shapes=[pltpu.VMEM((d, tf), BF16), pltpu.VMEM((d, tf), BF16)],
        ),
        out_shape=jax.ShapeDtypeStruct((n_blocks * r_blk, ff), BF16),
        compiler_params=_params(2),
        name="moe_up",
    )(meta, blk, exp, valid, xs, w_gu, w_gu, b_gu.reshape(n_exp, 1, ff2), b_gu.reshape(n_exp, 1, ff2))


def _moe_down_kernel(meta_ref, blk_ref, exp_ref, valid_ref, a_ref, w_ref, b_ref, o_ref, wb):
    i = pl.program_id(0)

    def rows(r0, n):
        o_ref[pl.ds(r0, n), :] = _dot(a_ref[pl.ds(r0, n), :], wb[...]) + b_ref[0]

    def cast_weights():
        wb[...] = w_ref[0].astype(BF16)

    @pl.when(i < meta_ref[0])
    def _():
        _moe_block_rows(valid_ref[i], a_ref.shape[0], cast_weights, rows, o_ref)


def _moe_down(meta, blk, exp, valid, act, w_down, b_down, n_blocks, tn=512):
    n_exp, ff, d = w_down.shape
    r_blk = MOE_BLOCK_ROWS
    nj = d // tn
    assert d % tn == 0

    def jj(i, j, mt):
        return jnp.where(i < mt[0], j, nj - 1)

    return pl.pallas_call(
        _moe_down_kernel,
        grid_spec=pltpu.PrefetchScalarGridSpec(
            num_scalar_prefetch=4,
            grid=(n_blocks, nj),
            in_specs=[pl.BlockSpec((r_blk, ff), lambda i, j, mt, bk, ex, vd: (bk[i], 0)),
                      pl.BlockSpec((1, ff, tn), lambda i, j, mt, bk, ex, vd: (ex[i], 0, jj(i, j, mt))),
                      pl.BlockSpec((1, 1, tn), lambda i, j, mt, bk, ex, vd: (ex[i], 0, jj(i, j, mt)))],
            out_specs=pl.BlockSpec((r_blk, tn), lambda i, j, mt, bk, ex, vd: (bk[i], jj(i, j, mt))),
            scratch_shapes=[pltpu.VMEM((ff, tn), BF16)],
        ),
        out_shape=jax.ShapeDtypeStruct((n_blocks * r_blk, d), F32),
        compiler_params=_params(2),
        name="moe_down",
    )(meta, blk, exp, valid, act, w_down, b_down.reshape(n_exp, 1, d))


def _moe_combine_kernel(dcur_ref, dnext_ref, h_ref, gate_ref, eo_ref, y_ref, rows, sem, *, tc, nt):
    t = pl.program_id(0)
    slot = t % 2

    def row_copy(src_row, s, k, q):
        return pltpu.make_async_copy(eo_ref.at[pl.ds(src_row, 1), :], rows.at[s, k, pl.ds(q, 1), :], sem.at[s])

    def issue(dref, s):
        def body(q, carry):
            for k in range(TOP_K):
                row_copy(dref[0, 0, q * TOP_K + k], s, k, q).start()
            return carry

        lax.fori_loop(0, tc, body, 0)

    @pl.when(t == 0)
    def _():
        issue(dcur_ref, 0)

    @pl.when(t + 1 < nt)
    def _():
        issue(dnext_ref, 1 - slot)

    def drain(q, carry):
        for k in range(TOP_K):
            row_copy(0, slot, k, 0).wait()
        return carry

    lax.fori_loop(0, tc, drain, 0)
    moe = jnp.zeros(h_ref.shape, F32)
    for k in range(TOP_K):
        moe = moe + gate_ref[:, k:k + 1] * rows[slot, k]
    y_ref[...] = h_ref[...] + moe


def _moe_combine(dest, h, gate, expert_out):
    m, d = h.shape
    tc = min(m, 128)
    assert m % tc == 0
    nt = m // tc
    dest3 = dest.reshape(nt, 1, tc * TOP_K)
    return pl.pallas_call(
        functools.partial(_moe_combine_kernel, tc=tc, nt=nt),
        grid=(nt,),
        in_specs=[pl.BlockSpec((1, 1, tc * TOP_K), lambda i: (i, 0, 0), memory_space=pltpu.SMEM),
                  pl.BlockSpec((1, 1, tc * TOP_K), lambda i: (jnp.minimum(i + 1, nt - 1), 0, 0),
                               memory_space=pltpu.SMEM),
                  pl.BlockSpec((tc, d), lambda i: (i, 0)),
                  pl.BlockSpec((tc, LANE), lambda i: (i, 0)),
                  pl.BlockSpec(memory_space=pl.ANY)],
        out_specs=pl.BlockSpec((tc, d), lambda i: (i, 0)),
        out_shape=jax.ShapeDtypeStruct((m, d), F32),
        scratch_shapes=[pltpu.VMEM((2, TOP_K, tc, d), F32), pltpu.SemaphoreType.DMA((2,))],
        compiler_params=_params(1),
        name="moe_combine",
    )(dest3, dest3, h, gate, expert_out)


def _moe_tables(idx, n_exp, n_blocks):
    r_blk = MOE_BLOCK_ROWS
    e_flat = idx.reshape(-1)
    n_asg = e_flat.shape[0]
    onehot = (e_flat[:, None] == jnp.arange(n_exp, dtype=I32)[None, :]).astype(I32)
    csum = jnp.cumsum(onehot, axis=0)
    counts = csum[-1]
    pos = jnp.take_along_axis(csum, e_flat[:, None], axis=1)[:, 0] - 1
    nblk = (counts + r_blk - 1) // r_blk
    blk_end = jnp.cumsum(nblk)
    first = blk_end - nblk
    used = blk_end[-1]
    slot = (first[e_flat] + pos // r_blk) * r_blk + pos % r_blk
    src = jnp.zeros((n_blocks * r_blk,), I32).at[slot].set(jnp.arange(n_asg, dtype=I32) // TOP_K)
    bid = jnp.minimum(jnp.arange(n_blocks, dtype=I32), used - 1)
    exp = jnp.minimum(jnp.sum((blk_end[None, :] <= bid[:, None]).astype(I32), axis=1), n_exp - 1)
    valid = jnp.clip(counts[exp] - (bid - first[exp]) * r_blk, 0, r_blk).astype(I32)
    valid = jnp.where(jnp.arange(n_blocks) < used, valid, 0)
    meta = used.reshape(1).astype(I32)
    return meta, bid, exp, valid, src.reshape(n_blocks, 1, r_blk), slot.astype(I32)


def kernel(x_prompt, x_sample, cache_k, cache_v, cache_mem_k, cache_mem_v, state_conv, page_table,
           mem_prompt, norm_mix_g, w_in, q_norm_g, k_norm_g, w_attn_o, conv_dw_w, conv_dw_b,
           conv_ln_g, conv_ln_b, w_conv_o, mem_norm_g, w_mem_kv, mq_norm_g, mk_norm_g, w_mem_o,
           w_out, norm_ffn_g, w_router, b_router, w_gu, b_gu, w_down, b_down):
    depth = w_in.shape[0]
    assert depth == 1 and x_sample.shape[1] == 1
    bp, sp, d = x_prompt.shape
    bs = x_sample.shape[0]
    ch = conv_dw_w.shape[-1]
    hd = q_norm_g.shape[-1]
    n_kv = cache_k.shape[3]
    kvw = n_kv * hd
    qw = w_attn_o.shape[1]
    n_heads = qw // hd
    mqw = w_mem_o.shape[1]
    mem_heads, mhd = cache_mem_k.shape[3], cache_mem_k.shape[4]
    n_mem = cache_mem_k.shape[2]
    n_exp = w_router.shape[-1]
    o_ub, o_q = ch, 2 * ch
    o_k = o_q + qw
    o_v = o_k + kvw
    o_qm = o_v + kvw
    o_g = o_qm + mqw
    n_in = w_in.shape[-1]
    assert n_in == o_g + 3 * d and o_ub == ch
    slopes = jnp.asarray(2.0 ** (-8.0 * np.arange(1, n_heads + 1) / n_heads), dtype=F32)
    l = 0
    tp = bp * sp

    xp = x_prompt.reshape(tp, d)
    xn_p = _rmsnorm(xp, norm_mix_g[l], BF16)
    proj_p = _mm(xn_p, w_in[l])
    act_p, conv_p = _conv_prompt(proj_p, bp, sp, ch, conv_dw_w[l], conv_dw_b[l], conv_ln_g[l], conv_ln_b[l])
    kn_p, kmean_p = _headnorm(proj_p, o_k, kvw, k_norm_g[l], rows_per_block=MOBA_BLOCK, with_mean=True)
    attn_p = _moba_prompt(proj_p, kn_p, kmean_p.reshape(bp, sp // MOBA_BLOCK, kvw), q_norm_g[l], slopes,
                          bp, sp, o_q, o_v, n_heads, n_kv)
    memn = _rmsnorm(mem_prompt.reshape(bp * n_mem, d), mem_norm_g[l], BF16)
    mem_kv = _mm(memn, w_mem_kv[l])
    mk_p = _headnorm(mem_kv, 0, mqw, mk_norm_g[l], rows_per_block=256, with_mean=False)
    mv_p = mem_kv[:, mqw:]
    memo_p = _mem_attn(proj_p.reshape(bp, sp, n_in), o_qm, mk_p.reshape(bp, n_mem, mqw),
                       mv_p.reshape(bp, n_mem, mqw), mq_norm_g[l], hi=False, out_dtype=BF16)
    merged_p = _merge(act_p, attn_p, memo_p.reshape(tp, mqw), w_conv_o[l], w_attn_o[l], w_mem_o[l],
                      proj_p, o_g, hi=False, out_dtype=BF16)
    h_p = _mm(merged_p, w_out[l], residual=xp)
    hn_all, idx_p, gate_p = _router(h_p, norm_ffn_g[l], w_router[l], b_router[l], total_rows=tp + bs)

    xs = x_sample.reshape(bs, d)
    xn_s = _rmsnorm(xs, norm_mix_g[l], F32)
    proj_s = _mm(xn_s, w_in[l], hi=True)
    proj_s3 = proj_s.reshape(bs, 1, n_in)
    act_s, conv_s = _conv_sample(proj_s3, state_conv[l], conv_dw_w[l], conv_dw_b[l], conv_ln_g[l], conv_ln_b[l])
    kn_s = _headnorm(proj_s, o_k, kvw, k_norm_g[l], rows_per_block=bs, with_mean=False)
    v_s = proj_s[:, o_v:o_v + kvw]
    attn_s = _moba_sample(page_table, proj_s[:, o_q:o_q + qw].reshape(bs, n_heads, hd),
                          kn_s.reshape(bs, n_kv, hd), v_s.reshape(bs, n_kv, hd), q_norm_g[l], slopes,
                          cache_k[l], cache_v[l])
    memo_s = _mem_attn(proj_s3, o_qm, cache_mem_k[l].reshape(bs, n_mem, mqw),
                       cache_mem_v[l].reshape(bs, n_mem, mqw), mq_norm_g[l], hi=True, out_dtype=F32)
    merged_s = _merge(act_s.reshape(bs, ch), attn_s.reshape(bs, qw), memo_s.reshape(bs, mqw),
                      w_conv_o[l], w_attn_o[l], w_mem_o[l], proj_s, o_g, hi=True, out_dtype=F32)
    h_s = _mm(merged_s, w_out[l], residual=xs, hi=True)
    hn_all, idx_s, gate_s = _router(h_s, norm_ffn_g[l], w_router[l], b_router[l], total_rows=tp + bs,
                                    row_off=tp, hn_into=hn_all)

    n_asg = (tp + bs) * TOP_K
    n_blocks = n_asg // MOE_BLOCK_ROWS + n_exp
    idx_all = jnp.concatenate([idx_p[:, :TOP_K], idx_s[:, :TOP_K]], axis=0)
    meta, bid, exp, valid, src, slot = _moe_tables(idx_all, n_exp, n_blocks)
    x_sorted = _moe_gather(meta, bid, valid, src, hn_all, n_blocks)
    act_e = _moe_up(meta, bid, exp, valid, x_sorted, w_gu[l], b_gu[l], n_blocks)
    out_e = _moe_down(meta, bid, exp, valid, act_e, w_down[l], b_down[l], n_blocks)
    y_p = _moe_combine(slot[:tp * TOP_K], h_p, gate_p, out_e)
    y_s = _moe_combine(slot[tp * TOP_K:], h_s, gate_s, out_e)

    return (y_p.reshape(bp, sp, d), y_s.reshape(bs, 1, d),
            kn_p.reshape(depth, bp, sp, n_kv, hd),
            proj_p[:, o_v:o_v + kvw].reshape(depth, bp, sp, n_kv, hd),
            kn_s.reshape(depth, bs, 1, n_kv, hd), v_s.reshape(depth, bs, 1, n_kv, hd),
            mk_p.reshape(depth, bp, n_mem, mem_heads, mhd), mv_p.reshape(depth, bp, n_mem, mem_heads, mhd),
            conv_p.reshape(depth, bp, conv_p.shape[1], ch), conv_s.reshape(depth, bs, conv_s.shape[1], ch))
```

```python
import functools

import numpy as np
import jax
import jax.numpy as jnp
from jax import lax
from jax.experimental import pallas as pl
from jax.experimental.pallas import tpu as pltpu

F32 = jnp.float32
BF16 = jnp.bfloat16
I32 = jnp.int32

EPS = 1e-6
MOBA_BLOCK = 256
MOBA_TOP = 3
TOP_K = 4
SWIGLU_LIMIT = 7.0
SWIGLU_ALPHA = 1.702
CONV_HALO = 32

V7X_VMEM_LIMIT_BYTES = 56 * 1024 * 1024
LANE = 128

MOE_SUB = 256
MOE_BLOCK_ROWS = 5 * MOE_SUB
KV_CHUNK_PAGES = 8


def _params(n_axes):
    return pltpu.CompilerParams(dimension_semantics=("arbitrary",) * n_axes,
                                vmem_limit_bytes=V7X_VMEM_LIMIT_BYTES)


def _split_bf16(a):
    hi = a.astype(BF16)
    lo = (a - hi.astype(F32)).astype(BF16)
    return hi, lo


def _dot(a, b, dims=(((1,), (0,)), ((), ()))):
    return lax.dot_general(a, b, dims, preferred_element_type=F32)


def _dot_hi(a, b, dims=(((1,), (0,)), ((), ()))):
    ah, al = _split_bf16(a)
    bh, bl = _split_bf16(b)
    return _dot(ah, bh, dims) + (_dot(ah, bl, dims) + _dot(al, bh, dims))


_NT = (((1,), (1,)), ((), ()))


def _rms(x, g):
    return x * lax.rsqrt(jnp.mean(x * x, axis=-1, keepdims=True) + EPS) * g


def _rmsnorm_kernel(x_ref, g_ref, o_ref):
    o_ref[...] = _rms(x_ref[...], g_ref[...]).astype(o_ref.dtype)


def _rmsnorm(x, g, out_dtype):
    m, d = x.shape
    tm = min(m, 256)
    assert m % tm == 0
    return pl.pallas_call(
        _rmsnorm_kernel,
        grid=(m // tm,),
        in_specs=[pl.BlockSpec((tm, d), lambda i: (i, 0)),
                  pl.BlockSpec((1, d), lambda i: (0, 0))],
        out_specs=pl.BlockSpec((tm, d), lambda i: (i, 0)),
        out_shape=jax.ShapeDtypeStruct((m, d), out_dtype),
        compiler_params=_params(1),
        name="rmsnorm",
    )(x, g.reshape(1, d))


def _mm_kernel(x_ref, w_ref, *rest, hi, has_res):
    o_ref = rest[-1]
    if hi:
        acc = _dot_hi(x_ref[...], w_ref[...])
    else:
        acc = _dot(x_ref[...], w_ref[...].astype(BF16))
    if has_res:
        acc = rest[0][...] + acc
    o_ref[...] = acc


def _mm(x, w, *, residual=None, hi=False, tn=512):
    m, k = x.shape
    n = w.shape[1]
    tm = min(m, 1024)
    assert m % tm == 0 and n % tn == 0
    in_specs = [pl.BlockSpec((tm, k), lambda i, j: (i, 0)),
                pl.BlockSpec((k, tn), lambda i, j: (0, j))]
    args = [x, w]
    if residual is not None:
        in_specs.append(pl.BlockSpec((tm, tn), lambda i, j: (i, j)))
        args.append(residual)
    return pl.pallas_call(
        functools.partial(_mm_kernel, hi=hi, has_res=residual is not None),
        grid=(m // tm, n // tn),
        in_specs=in_specs,
        out_specs=pl.BlockSpec((tm, tn), lambda i, j: (i, j)),
        out_shape=jax.ShapeDtypeStruct((m, n), F32),
        compiler_params=_params(2),
        name="matmul",
    )(*args)


def _headnorm_kernel(x_ref, g_ref, o_ref, *mean_ref, hd, nh):
    for h in range(nh):
        y = _rms(x_ref[:, h * hd:(h + 1) * hd], g_ref[...])
        o_ref[:, h * hd:(h + 1) * hd] = y
        if mean_ref:
            mean_ref[0][0, :, h * hd:(h + 1) * hd] = jnp.mean(y, axis=0, keepdims=True)


def _headnorm(x, col_off, width, g, *, rows_per_block, with_mean):
    m = x.shape[0]
    hd = g.shape[-1]
    tm = min(m, rows_per_block)
    assert m % tm == 0 and col_off % width == 0 and width % hd == 0
    cb = col_off // width
    out_shape = [jax.ShapeDtypeStruct((m, width), F32)]
    out_specs = [pl.BlockSpec((tm, width), lambda i: (i, 0))]
    if with_mean:
        out_shape.append(jax.ShapeDtypeStruct((m // tm, 1, width), F32))
        out_specs.append(pl.BlockSpec((1, 1, width), lambda i: (i, 0, 0)))
    res = pl.pallas_call(
        functools.partial(_headnorm_kernel, hd=hd, nh=width // hd),
        grid=(m // tm,),
        in_specs=[pl.BlockSpec((tm, width), lambda i: (i, cb)),
                  pl.BlockSpec((1, hd), lambda i: (0, 0))],
        out_specs=out_specs,
        out_shape=out_shape,
        compiler_params=_params(1),
        name="headnorm",
    )(x, g.reshape(1, hd))
    return res if with_mean else res[0]


def _conv_prompt_kernel(ua_ref, ub_ref, w_ref, b_ref, g_ref, beta_ref, act_ref, new_ref,
                        full_ref, conv_ref, *, tt, nt, width, ch):
    t = pl.program_id(1)
    hist = width - 1
    lead = CONV_HALO - hist

    @pl.when(t == 0)
    def _():
        full_ref[0:CONV_HALO, :] = jnp.zeros((CONV_HALO, ch), F32)

    full_ref[CONV_HALO:CONV_HALO + tt, :] = ua_ref[...] * jax.nn.sigmoid(ub_ref[...])

    rc, cc = 32, 256
    for r in range(tt // rc):
        for c in range(ch // cc):
            cs = slice(c * cc, (c + 1) * cc)
            acc = jnp.zeros((rc, cc), F32)
            for w in range(width):
                r0 = lead + w + r * rc
                acc = acc + full_ref[r0:r0 + rc, cs] * w_ref[w:w + 1, cs]
            conv_ref[r * rc:(r + 1) * rc, cs] = acc + b_ref[:, cs]

    for r in range(tt // rc):
        cf = conv_ref[r * rc:(r + 1) * rc, :]
        mu = jnp.mean(cf, axis=-1, keepdims=True)
        d = cf - mu
        var = jnp.mean(d * d, axis=-1, keepdims=True)
        cn = d * lax.rsqrt(var + EPS) * g_ref[...] + beta_ref[...]
        act_ref[r * rc:(r + 1) * rc, :] = (cn * jax.nn.sigmoid(cn)).astype(act_ref.dtype)

    tail = full_ref[tt + lead:tt + CONV_HALO, :]
    full_ref[lead:CONV_HALO, :] = tail

    @pl.when(t == nt - 1)
    def _():
        new_ref[0] = tail


def _conv_prompt(proj, b, s, ch, dw_w, dw_b, ln_g, ln_b):
    width = dw_w.shape[0]
    tt = 256
    assert s % tt == 0 and ch % 256 == 0 and width - 1 <= CONV_HALO
    nt = s // tt
    row = lambda bi, ti: (bi * nt + ti, 0)
    vec = pl.BlockSpec((1, ch), lambda bi, ti: (0, 0))
    return pl.pallas_call(
        functools.partial(_conv_prompt_kernel, tt=tt, nt=nt, width=width, ch=ch),
        grid=(b, nt),
        in_specs=[pl.BlockSpec((tt, ch), row),
                  pl.BlockSpec((tt, ch), lambda bi, ti: (bi * nt + ti, 1)),
                  pl.BlockSpec((width, ch), lambda bi, ti: (0, 0)),
                  vec, vec, vec],
        out_specs=[pl.BlockSpec((tt, ch), row),
                   pl.BlockSpec((1, width - 1, ch), lambda bi, ti: (bi, 0, 0))],
        out_shape=[jax.ShapeDtypeStruct((b * s, ch), BF16),
                   jax.ShapeDtypeStruct((b, width - 1, ch), F32)],
        scratch_shapes=[pltpu.VMEM((CONV_HALO + tt, ch), F32),
                        pltpu.VMEM((tt, ch), F32)],
        compiler_params=_params(2),
        name="conv_prompt",
    )(proj, proj, dw_w, dw_b.reshape(1, ch), ln_g.reshape(1, ch), ln_b.reshape(1, ch))


def _conv_sample_kernel(ua_ref, ub_ref, hist_ref, w_ref, b_ref, g_ref, beta_ref,
                        act_ref, new_ref, *, width):
    hist = width - 1
    u = ua_ref[0] * jax.nn.sigmoid(ub_ref[0])
    h = hist_ref[0]
    cf = (jnp.sum(h * w_ref[0:hist, :], axis=0, keepdims=True)
          + u * w_ref[hist:width, :] + b_ref[...])
    mu = jnp.mean(cf, axis=-1, keepdims=True)
    d = cf - mu
    var = jnp.mean(d * d, axis=-1, keepdims=True)
    cn = d * lax.rsqrt(var + EPS) * g_ref[...] + beta_ref[...]
    act_ref[0] = cn * jax.nn.sigmoid(cn)
    new_ref[0, 0:hist - 1, :] = hist_ref[0, 1:hist, :]
    new_ref[0, hist - 1:hist, :] = u


def _conv_sample(proj3, state, dw_w, dw_b, ln_g, ln_b):
    b, hist, ch = state.shape
    width = dw_w.shape[0]
    vec = pl.BlockSpec((1, ch), lambda bi: (0, 0))
    return pl.pallas_call(
        functools.partial(_conv_sample_kernel, width=width),
        grid=(b,),
        in_specs=[pl.BlockSpec((1, 1, ch), lambda bi: (bi, 0, 0)),
                  pl.BlockSpec((1, 1, ch), lambda bi: (bi, 0, 1)),
                  pl.BlockSpec((1, hist, ch), lambda bi: (bi, 0, 0)),
                  pl.BlockSpec((width, ch), lambda bi: (0, 0)),
                  vec, vec, vec],
        out_specs=[pl.BlockSpec((1, 1, ch), lambda bi: (bi, 0, 0)),
                   pl.BlockSpec((1, hist, ch), lambda bi: (bi, 0, 0))],
        out_shape=[jax.ShapeDtypeStruct((b, 1, ch), F32),
                   jax.ShapeDtypeStruct((b, hist, ch), F32)],
        compiler_params=_params(1),
        name="conv_sample",
    )(proj3, proj3, state, dw_w, dw_b.reshape(1, ch), ln_g.reshape(1, ch), ln_b.reshape(1, ch))


def _rank_select(gate, lane, n_valid, n_top, n_cols):
    neg = jnp.float32(-jnp.inf)
    gm = jnp.where(lane < n_valid, gate, neg)
    rank = jnp.zeros(gate.shape, F32)
    for m in range(n_cols):
        gcol = gm[:, m:m + 1]
        beats = (gcol > gm) | ((gcol == gm) & (m < lane))
        rank = rank + jnp.where(beats, 1.0, 0.0)
    return jnp.where((rank < n_top) & (lane < n_valid), 1.0, 0.0)


def _rank_select_rows(gate, n_valid, n_top):
    neg = jnp.float32(-jnp.inf)
    rid = lax.broadcasted_iota(I32, gate.shape, 0)
    gm = jnp.where(rid < n_valid, gate, neg)
    rank = jnp.zeros(gate.shape, F32)
    for m in range(gate.shape[0]):
        grow = gm[m:m + 1, :]
        beats = (grow > gm) | ((grow == gm) & (m < rid))
        rank = rank + jnp.where(beats, 1.0, 0.0)
    return jnp.where((rank < n_top) & (rid < n_valid), 1.0, 0.0)


def _moba_prompt_kernel(slopes_ref, q_ref, k_ref, v_ref, km_ref, gq_ref, o_ref, kb, vb,
                        *, nb, blk, group, hd):
    kvh = pl.program_id(1)
    i = pl.program_id(2)
    scale = hd ** -0.5
    neg = jnp.float32(-jnp.inf)

    @pl.when(i == 0)
    def _():
        kb[...] = k_ref[...].astype(BF16)
        vb[...] = v_ref[...].astype(BF16)

    km = km_ref[0]
    rows = group * blk
    row = lax.broadcasted_iota(I32, (blk, blk), 0)
    col = lax.broadcasted_iota(I32, (blk, blk), 1)
    rc1 = (row - col).astype(F32)
    rc = jnp.concatenate([rc1] * group, axis=0)
    lane = lax.broadcasted_iota(I32, (rows, LANE), 1)
    slope = jnp.concatenate([jnp.full((blk, 1), slopes_ref[kvh * group + hh], F32)
                             for hh in range(group)], axis=0)
    qn = jnp.concatenate([_rms(q_ref[:, hh * hd:(hh + 1) * hd], gq_ref[...])
                          for hh in range(group)], axis=0)
    sel_t = _rank_select_rows(_dot_hi(km, qn, _NT), i, MOBA_TOP)
    sel = jnp.transpose(jnp.concatenate([sel_t, jnp.zeros((LANE - nb, rows), F32)], axis=0))
    qb = qn.astype(BF16)
    i0 = pl.multiple_of(i * blk, blk)
    s = jnp.where(rc >= 0.0, _dot(qb, kb[pl.ds(i0, blk), :], _NT) * scale - slope * rc, neg)
    m0 = jnp.max(s, axis=-1, keepdims=True)
    p = jnp.exp(s - m0)
    l0 = jnp.sum(p, axis=-1, keepdims=True)
    acc0 = _dot(p.astype(BF16), vb[pl.ds(i0, blk), :])

    def past_block(j, carry):
        m, l, acc = carry
        j0 = pl.multiple_of(j * blk, blk)
        picked = jnp.sum(jnp.where(lane == j, sel, 0.0), axis=-1, keepdims=True)
        dist = rc + ((i - j) * blk).astype(F32)
        sj = _dot(qb, kb[pl.ds(j0, blk), :], _NT) * scale - slope * dist
        sj = jnp.where(jnp.broadcast_to(picked, (rows, blk)) > 0.5, sj, neg)
        m_new = jnp.maximum(m, jnp.max(sj, axis=-1, keepdims=True))
        alpha = jnp.exp(m - m_new)
        pj = jnp.exp(sj - m_new)
        l_new = alpha * l + jnp.sum(pj, axis=-1, keepdims=True)
        acc_new = alpha * acc + _dot(pj.astype(BF16), vb[pl.ds(j0, blk), :])
        return m_new, l_new, acc_new

    _, l, acc = lax.fori_loop(0, i, past_block, (m0, l0, acc0))
    out = acc / l
    for hh in range(group):
        o_ref[:, hh * hd:(hh + 1) * hd] = out[hh * blk:(hh + 1) * blk, :].astype(o_ref.dtype)


def _moba_prompt(proj, kn, kmean, gq, slopes, b, s, o_q, o_v, n_heads, n_kv):
    hd = gq.shape[-1]
    blk = MOBA_BLOCK
    nb = s // blk
    group = n_heads // n_kv
    gw = group * hd
    assert s % blk == 0 and o_q % gw == 0 and o_v % hd == 0 and nb <= LANE
    return pl.pallas_call(
        functools.partial(_moba_prompt_kernel, nb=nb, blk=blk, group=group, hd=hd),
        grid_spec=pltpu.PrefetchScalarGridSpec(
            num_scalar_prefetch=1,
            grid=(b, n_kv, nb),
            in_specs=[pl.BlockSpec((blk, gw), lambda bi, g, i, sl: (bi * nb + i, o_q // gw + g)),
                      pl.BlockSpec((s, hd), lambda bi, g, i, sl: (bi, g)),
                      pl.BlockSpec((s, hd), lambda bi, g, i, sl: (bi, o_v // hd + g)),
                      pl.BlockSpec((1, nb, hd), lambda bi, g, i, sl: (bi, 0, g)),
                      pl.BlockSpec((1, hd), lambda bi, g, i, sl: (0, 0))],
            out_specs=pl.BlockSpec((blk, gw), lambda bi, g, i, sl: (bi * nb + i, g)),
            scratch_shapes=[pltpu.VMEM((s, hd), BF16), pltpu.VMEM((s, hd), BF16)],
        ),
        out_shape=jax.ShapeDtypeStruct((b * s, n_heads * hd), BF16),
        compiler_params=_params(3),
        name="moba_prompt",
    )(slopes, proj, kn, proj, kmean, gq.reshape(1, hd))


def _moba_sample_kernel(pt_ref, q_ref, kn_ref, vn_ref, gq_ref, slope_ref, ck_ref, cv_ref, o_ref,
                        buf, sem, s_ref, *, n_pages, page, n_kv, hd, blk):
    b = pl.program_id(0)
    n_heads = q_ref.shape[1]
    group = n_heads // n_kv
    ppc = KV_CHUNK_PAGES
    rpp = page * n_kv
    ch = ppc * rpp
    kpc = ppc * page
    n_chunks = n_pages // ppc
    past = n_pages * page
    n_blk = past // blk
    bpc = kpc // blk
    cpb = blk * n_kv
    kv_shift = n_kv.bit_length() - 1
    scale = hd ** -0.5
    neg = jnp.float32(-jnp.inf)

    def copies(step):
        src = ck_ref if step < n_chunks else cv_ref
        c = step % n_chunks
        slot = step % 2
        return [pltpu.make_async_copy(src.at[pt_ref[b, c * ppc + p]],
                                      buf.at[slot, pl.ds(p * rpp, rpp), :],
                                      sem.at[slot]) for p in range(ppc)]

    def start(step):
        for cp in copies(step):
            cp.start()

    def wait(step):
        for cp in copies(step):
            cp.wait()

    qn = _rms(q_ref[0], gq_ref[...])
    head_grp = lax.broadcasted_iota(I32, (n_heads, 1), 0) // group
    lane_b = lax.broadcasted_iota(I32, (n_heads, LANE), 1)
    gates = jnp.zeros((n_heads, LANE), F32)
    col = lax.broadcasted_iota(I32, (n_heads, ch), 1)
    mine = jnp.bitwise_and(col, n_kv - 1) == head_grp
    key = jnp.right_shift(col, kv_shift)

    start(0)
    for c in range(n_chunks):
        start(c + 1)
        wait(c)
        sc = _dot_hi(qn, buf[c % 2], _NT)
        s_ref[:, c * ch:(c + 1) * ch] = sc
        own_sc = jnp.where(mine, sc, 0.0)
        for k in range(bpc):
            gsum = jnp.sum(own_sc[:, k * cpb:(k + 1) * cpb], axis=-1, keepdims=True) * (1.0 / blk)
            gates = jnp.where(lane_b == c * bpc + k, gsum, gates)

    sel = _rank_select(gates, lane_b, n_blk, min(MOBA_TOP, n_blk + 1), n_blk)
    self_all = _dot_hi(qn, kn_ref[0], _NT)
    grp_lane = lax.broadcasted_iota(I32, (n_heads, n_kv), 1)
    s_self = jnp.sum(jnp.where(grp_lane == head_grp, self_all, 0.0), axis=-1, keepdims=True) * scale
    slope = slope_ref[...]
    mx = s_self
    for c in range(n_chunks):
        dist = (past - c * kpc - key).astype(F32)
        sc = s_ref[:, c * ch:(c + 1) * ch] * scale - slope * dist
        keep = jnp.concatenate(
            [jnp.broadcast_to(
                jnp.sum(jnp.where(lane_b == c * bpc + k, sel, 0.0), axis=-1, keepdims=True),
                (n_heads, cpb)) for k in range(bpc)], axis=-1)
        sc = jnp.where((keep > 0.5) & mine, sc, neg)
        s_ref[:, c * ch:(c + 1) * ch] = sc
        mx = jnp.maximum(mx, jnp.max(sc, axis=-1, keepdims=True))

    p_self = jnp.exp(s_self - mx)
    den = p_self
    acc = jnp.zeros((n_heads, hd), F32)
    for c in range(n_chunks):
        step = n_chunks + c
        if step + 1 < 2 * n_chunks:
            start(step + 1)
        wait(step)
        p = jnp.exp(s_ref[:, c * ch:(c + 1) * ch] - mx)
        den = den + jnp.sum(p, axis=-1, keepdims=True)
        acc = acc + _dot_hi(p, buf[step % 2])
    v_self = jnp.zeros((n_heads, hd), F32)
    for g in range(n_kv):
        v_self = jnp.where(head_grp == g, vn_ref[0, g:g + 1, :], v_self)
    o_ref[0] = (acc + p_self * v_self) / den


def _moba_sample(page_table, q3, kn3, vn3, gq, slopes, cache_k, cache_v):
    b, n_heads, hd = q3.shape
    n_kv = kn3.shape[1]
    n_pages = page_table.shape[1]
    n_pool, page = cache_k.shape[0], cache_k.shape[1]
    assert (n_pages * page) % MOBA_BLOCK == 0 and n_pages % KV_CHUNK_PAGES == 0
    assert (KV_CHUNK_PAGES * page) % MOBA_BLOCK == 0 and (n_pages * page) // MOBA_BLOCK <= LANE
    assert n_kv & (n_kv - 1) == 0
    ch = KV_CHUNK_PAGES * page * n_kv
    hbm = pl.BlockSpec(memory_space=pl.ANY)
    return pl.pallas_call(
        functools.partial(_moba_sample_kernel, n_pages=n_pages, page=page, n_kv=n_kv, hd=hd,
                          blk=MOBA_BLOCK),
        grid_spec=pltpu.PrefetchScalarGridSpec(
            num_scalar_prefetch=1,
            grid=(b,),
            in_specs=[pl.BlockSpec((1, n_heads, hd), lambda bi, pt: (bi, 0, 0)),
                      pl.BlockSpec((1, n_kv, hd), lambda bi, pt: (bi, 0, 0)),
                      pl.BlockSpec((1, n_kv, hd), lambda bi, pt: (bi, 0, 0)),
                      pl.BlockSpec((1, hd), lambda bi, pt: (0, 0)),
                      pl.BlockSpec((n_heads, 1), lambda bi, pt: (0, 0)),
                      hbm, hbm],
            out_specs=pl.BlockSpec((1, n_heads, hd), lambda bi, pt: (bi, 0, 0)),
            scratch_shapes=[pltpu.VMEM((2, ch, hd), F32),
                            pltpu.SemaphoreType.DMA((2,)),
                            pltpu.VMEM((n_heads, n_pages * page * n_kv), F32)],
        ),
        out_shape=jax.ShapeDtypeStruct((b, n_heads, hd), F32),
        compiler_params=_params(1),
        name="moba_sample",
    )(page_table, q3, kn3, vn3, gq.reshape(1, hd), slopes.reshape(n_heads, 1),
      cache_k.reshape(n_pool, page * n_kv, hd), cache_v.reshape(n_pool, page * n_kv, hd))


def _mem_attn_kernel(q_ref, mk_ref, mv_ref, g_ref, o_ref, *, nh, hd, hi):
    scale = hd ** -0.5
    tq = q_ref.shape[1]
    for h in range(nh):
        cs = slice(h * hd, (h + 1) * hd)
        qh = _rms(q_ref[0, :, cs], g_ref[...])
        if tq < 8:
            qh = jnp.broadcast_to(qh[0:1, :], (8, hd))
        mk = mk_ref[0, :, cs]
        mv = mv_ref[0, :, cs]
        if hi:
            s = _dot_hi(qh, mk, _NT) * scale
        else:
            s = _dot(qh.astype(BF16), mk.astype(BF16), _NT) * scale
        p = jnp.exp(s - jnp.max(s, axis=-1, keepdims=True))
        den = jnp.sum(p, axis=-1, keepdims=True)
        o = (_dot_hi(p, mv) if hi else _dot(p.astype(BF16), mv.astype(BF16))) / den
        o_ref[0, :, cs] = o[0:tq, :].astype(o_ref.dtype)


def _mem_attn(q3, col_off, mk, mv, g, *, hi, out_dtype):
    b, tq_all, _ = q3.shape
    n_mem, width = mk.shape[1], mk.shape[2]
    hd = g.shape[-1]
    tq = min(tq_all, 512)
    assert tq_all % tq == 0 and col_off % width == 0 and (tq_all == 1 or tq_all % 8 == 0)
    cb = col_off // width
    return pl.pallas_call(
        functools.partial(_mem_attn_kernel, nh=width // hd, hd=hd, hi=hi),
        grid=(b, tq_all // tq),
        in_specs=[pl.BlockSpec((1, tq, width), lambda bi, t: (bi, t, cb)),
                  pl.BlockSpec((1, n_mem, width), lambda bi, t: (bi, 0, 0)),
                  pl.BlockSpec((1, n_mem, width), lambda bi, t: (bi, 0, 0)),
                  pl.BlockSpec((1, hd), lambda bi, t: (0, 0))],
        out_specs=pl.BlockSpec((1, tq, width), lambda bi, t: (bi, t, 0)),
        out_shape=jax.ShapeDtypeStruct((b, tq_all, width), out_dtype),
        compiler_params=_params(2),
        name="mem_attn",
    )(q3, mk, mv, g.reshape(1, hd))


def _merge_kernel(a_ref, t_ref, m_ref, wc_ref, wa_ref, wm_ref, g0_ref, g1_ref, g2_ref, o_ref, *, hi):
    if hi:
        c = _dot_hi(a_ref[...], wc_ref[...])
        a = _dot_hi(t_ref[...], wa_ref[...])
        m = _dot_hi(m_ref[...], wm_ref[...])
    else:
        c = _dot(a_ref[...], wc_ref[...].astype(BF16))
        a = _dot(t_ref[...], wa_ref[...].astype(BF16))
        m = _dot(m_ref[...], wm_ref[...].astype(BF16))
    merged = (jax.nn.sigmoid(g0_ref[...]) * c + jax.nn.sigmoid(g1_ref[...]) * a
              + jax.nn.sigmoid(g2_ref[...]) * m)
    o_ref[...] = merged.astype(o_ref.dtype)


def _merge(act, attn, mem, w_conv_o, w_attn_o, w_mem_o, proj, o_g, *, hi, out_dtype, tn=512):
    m = act.shape[0]
    d = w_conv_o.shape[1]
    tm = min(m, 1024)
    assert m % tm == 0 and d % tn == 0 and o_g % tn == 0
    lhs = lambda a: pl.BlockSpec((tm, a.shape[1]), lambda i, j: (i, 0))
    wsp = lambda w: pl.BlockSpec((w.shape[0], tn), lambda i, j: (0, j))
    gsp = lambda br: pl.BlockSpec((tm, tn), lambda i, j: (i, (o_g + br * d) // tn + j))
    return pl.pallas_call(
        functools.partial(_merge_kernel, hi=hi),
        grid=(m // tm, d // tn),
        in_specs=[lhs(act), lhs(attn), lhs(mem), wsp(w_conv_o), wsp(w_attn_o), wsp(w_mem_o),
                  gsp(0), gsp(1), gsp(2)],
        out_specs=pl.BlockSpec((tm, tn), lambda i, j: (i, j)),
        out_shape=jax.ShapeDtypeStruct((m, d), out_dtype),
        compiler_params=_params(2),
        name="merge",
    )(act, attn, mem, w_conv_o, w_attn_o, w_mem_o, proj, proj, proj)


def _router_kernel(h_ref, g_ref, wr_ref, br_ref, *rest, n_exp):
    hn_ref, idx_ref, gate_ref = rest[-3:]
    hn = _rms(h_ref[...], g_ref[...])
    hn_ref[...] = hn
    logits = _dot_hi(hn, wr_ref[...]) + br_ref[...]
    tm = logits.shape[0]
    neg = jnp.float32(-jnp.inf)
    lane = lax.broadcasted_iota(I32, (tm, n_exp), 1).astype(F32)
    out_lane = lax.broadcasted_iota(I32, (tm, LANE), 1)
    idx_out = jnp.zeros((tm, LANE), F32)
    val_out = jnp.full((tm, LANE), neg, F32)
    vals = logits
    for k in range(TOP_K):
        mx = jnp.max(vals, axis=-1, keepdims=True)
        am = jnp.min(jnp.where(vals == mx, lane, jnp.float32(n_exp)), axis=-1, keepdims=True)
        idx_out = jnp.where(out_lane == k, am, idx_out)
        val_out = jnp.where(out_lane == k, mx, val_out)
        vals = jnp.where(lane == am, neg, vals)
    e = jnp.exp(val_out - jnp.max(val_out, axis=-1, keepdims=True))
    gate_ref[...] = e / jnp.sum(e, axis=-1, keepdims=True)
    idx_ref[...] = idx_out.astype(I32)


def _router(h, g, w_router, b_router, *, total_rows, row_off=0, hn_into=None):
    m, d = h.shape
    n_exp = w_router.shape[1]
    tm = min(m, 256)
    assert m % tm == 0 and row_off % tm == 0
    off = row_off // tm
    row = lambda w: pl.BlockSpec((tm, w), lambda i: (i, 0))
    in_specs = [row(d), pl.BlockSpec((1, d), lambda i: (0, 0)),
                pl.BlockSpec((d, n_exp), lambda i: (0, 0)),
                pl.BlockSpec((1, n_exp), lambda i: (0, 0))]
    args = [h, g.reshape(1, d), w_router, b_router.reshape(1, n_exp)]
    aliases = {}
    if hn_into is not None:
        in_specs.append(pl.BlockSpec(memory_space=pl.ANY))
        args.append(hn_into)
        aliases = {len(args) - 1: 0}
    return pl.pallas_call(
        functools.partial(_router_kernel, n_exp=n_exp),
        grid=(m // tm,),
        in_specs=in_specs,
        out_specs=[pl.BlockSpec((tm, d), lambda i: (i + off, 0)), row(LANE), row(LANE)],
        out_shape=[jax.ShapeDtypeStruct((total_rows, d), F32),
                   jax.ShapeDtypeStruct((m, LANE), I32),
                   jax.ShapeDtypeStruct((m, LANE), F32)],
        input_output_aliases=aliases,
        compiler_params=_params(1),
        name="router",
    )(*args)


def _moe_gather_kernel(meta_ref, blk_ref, valid_ref, src_ref, hn_ref, o_ref, rows, sem):
    i = pl.program_id(0)
    valid = valid_ref[i]
    r_blk = rows.shape[0]
    grp = 8

    @pl.when(i == 0)
    def _():
        rows[...] = jnp.zeros(rows.shape, F32)

    def row_copy(tok, r):
        return pltpu.make_async_copy(hn_ref.at[pl.ds(tok, 1), :], rows.at[pl.ds(r, 1), :], sem)

    @pl.when(i < meta_ref[0])
    def _():
        n_grp = (valid + grp - 1) // grp

        def issue(g, carry):
            for u in range(grp):
                row_copy(src_ref[0, 0, g * grp + u], g * grp + u).start()
            return carry

        lax.fori_loop(0, n_grp, issue, 0)

        def drain(g, carry):
            for u in range(grp):
                row_copy(0, 0).wait()
            return carry

        lax.fori_loop(0, n_grp, drain, 0)
        rid = lax.broadcasted_iota(I32, (r_blk, 1), 0)
        o_ref[...] = jnp.where(rid < valid, rows[...], 0.0).astype(o_ref.dtype)


def _moe_gather(meta, blk, valid, src, hn, n_blocks):
    d = hn.shape[1]
    r_blk = MOE_BLOCK_ROWS
    assert r_blk % 8 == 0
    hbm = pl.BlockSpec(memory_space=pl.ANY)
    return pl.pallas_call(
        _moe_gather_kernel,
        grid_spec=pltpu.PrefetchScalarGridSpec(
            num_scalar_prefetch=3,
            grid=(n_blocks,),
            in_specs=[pl.BlockSpec((1, 1, r_blk), lambda i, mt, bk, vd: (i, 0, 0),
                                   memory_space=pltpu.SMEM),
                      hbm],
            out_specs=pl.BlockSpec((r_blk, d), lambda i, mt, bk, vd: (bk[i], 0)),
            scratch_shapes=[pltpu.VMEM((r_blk, d), F32), pltpu.SemaphoreType.DMA(())],
        ),
        out_shape=jax.ShapeDtypeStruct((n_blocks * r_blk, d), BF16),
        compiler_params=_params(1),
        name="moe_gather",
    )(meta, blk, valid, src, hn)


def _moe_block_rows(valid, r_blk, cast_weights, rows, o_ref):
    half = MOE_SUB // 2
    n_full = valid // MOE_SUB
    rem = valid - n_full * MOE_SUB
    for k in range(r_blk // MOE_SUB + 1):
        @pl.when(n_full == k)
        def _(k=k):
            cast_weights()
            for s in range(k):
                rows(s * MOE_SUB, MOE_SUB)

    r_tail = pl.multiple_of(n_full * MOE_SUB, MOE_SUB)

    @pl.when(rem > half)
    def _():
        rows(r_tail, MOE_SUB)

    @pl.when((rem > 0) & (rem <= half))
    def _():
        rows(r_tail, half)

    def zero(s, carry):
        r0 = pl.multiple_of(s * half, half)
        o_ref[pl.ds(r0, half), :] = jnp.zeros((half, o_ref.shape[1]), o_ref.dtype)
        return carry

    lax.fori_loop((valid + half - 1) // half, r_blk // half, zero, 0)


def _moe_up_kernel(meta_ref, blk_ref, exp_ref, valid_ref, x_ref, wg_ref, wl_ref, bg_ref, bl_ref,
                   o_ref, wgb, wlb):
    i = pl.program_id(0)

    def rows(r0, n):
        x = x_ref[pl.ds(r0, n), :]
        glu = jnp.minimum(_dot(x, wgb[...]) + bg_ref[0], SWIGLU_LIMIT)
        lin = jnp.clip(_dot(x, wlb[...]) + bl_ref[0], -SWIGLU_LIMIT, SWIGLU_LIMIT)
        act = glu * jax.nn.sigmoid(SWIGLU_ALPHA * glu) * (lin + 1.0)
        o_ref[pl.ds(r0, n), :] = act.astype(o_ref.dtype)

    def cast_weights():
        wgb[...] = wg_ref[0].astype(BF16)
        wlb[...] = wl_ref[0].astype(BF16)

    @pl.when(i < meta_ref[0])
    def _():
        _moe_block_rows(valid_ref[i], x_ref.shape[0], cast_weights, rows, o_ref)


def _moe_up(meta, blk, exp, valid, xs, w_gu, b_gu, n_blocks, tf=256):
    n_exp, d, ff2 = w_gu.shape
    ff = ff2 // 2
    r_blk = MOE_BLOCK_ROWS
    nj = ff // tf
    assert ff % tf == 0

    def jj(i, j, mt):
        return jnp.where(i < mt[0], j, nj - 1)

    return pl.pallas_call(
        _moe_up_kernel,
        grid_spec=pltpu.PrefetchScalarGridSpec(
            num_scalar_prefetch=4,
            grid=(n_blocks, nj),
            in_specs=[pl.BlockSpec((r_blk, d), lambda i, j, mt, bk, ex, vd: (bk[i], 0)),
                      pl.BlockSpec((1, d, tf), lambda i, j, mt, bk, ex, vd: (ex[i], 0, jj(i, j, mt))),
                      pl.BlockSpec((1, d, tf), lambda i, j, mt, bk, ex, vd: (ex[i], 0, nj + jj(i, j, mt))),
                      pl.BlockSpec((1, 1, tf), lambda i, j, mt, bk, ex, vd: (ex[i], 0, jj(i, j, mt))),
                      pl.BlockSpec((1, 1, tf), lambda i, j, mt, bk, ex, vd: (ex[i], 0, nj + jj(i, j, mt)))],
            out_specs=pl.BlockSpec((r_blk, tf), lambda i, j, mt, bk, ex, vd: (bk[i], jj(i, j, mt))),
            scratch_shapes=[pltpu.VMEM((d, tf), BF16), pltpu.VMEM((d, tf), BF16)],
        ),
        out_shape=jax.ShapeDtypeStruct((n_blocks * r_blk, ff), BF16),
        compiler_params=_params(2),
        name="moe_up",
    )(meta, blk, exp, valid, xs, w_gu, w_gu, b_gu.reshape(n_exp, 1, ff2), b_gu.reshape(n_exp, 1, ff2))


def _moe_down_kernel(meta_ref, blk_ref, exp_ref, valid_ref, a_ref, w_ref, b_ref, o_ref, wb):
    i = pl.program_id(0)

    def rows(r0, n):
        o_ref[pl.ds(r0, n), :] = _dot(a_ref[pl.ds(r0, n), :], wb[...]) + b_ref[0]

    def cast_weights():
        wb[...] = w_ref[0].astype(BF16)

    @pl.when(i < meta_ref[0])
    def _():
        _moe_block_rows(valid_ref[i], a_ref.shape[0], cast_weights, rows, o_ref)


def _moe_down(meta, blk, exp, valid, act, w_down, b_down, n_blocks, tn=512):
    n_exp, ff, d = w_down.shape
    r_blk = MOE_BLOCK_ROWS
    nj = d // tn
    assert d % tn == 0

    def jj(i, j, mt):
        return jnp.where(i < mt[0], j, nj - 1)

    return pl.pallas_call(
        _moe_down_kernel,
        grid_spec=pltpu.PrefetchScalarGridSpec(
            num_scalar_prefetch=4,
            grid=(n_blocks, nj),
            in_specs=[pl.BlockSpec((r_blk, ff), lambda i, j, mt, bk, ex, vd: (bk[i], 0)),
                      pl.BlockSpec((1, ff, tn), lambda i, j, mt, bk, ex, vd: (ex[i], 0, jj(i, j, mt))),
                      pl.BlockSpec((1, 1, tn), lambda i, j, mt, bk, ex, vd: (ex[i], 0, jj(i, j, mt)))],
            out_specs=pl.BlockSpec((r_blk, tn), lambda i, j, mt, bk, ex, vd: (bk[i], jj(i, j, mt))),
            scratch_shapes=[pltpu.VMEM((ff, tn), BF16)],
        ),
        out_shape=jax.ShapeDtypeStruct((n_blocks * r_blk, d), F32),
        compiler_params=_params(2),
        name="moe_down",
    )(meta, blk, exp, valid, act, w_down, b_down.reshape(n_exp, 1, d))


def _moe_combine_kernel(dcur_ref, dnext_ref, h_ref, gate_ref, eo_ref, y_ref, rows, sem, *, tc, nt):
    t = pl.program_id(0)
    slot = t % 2

    def row_copy(src_row, s, k, q):
        return pltpu.make_async_copy(eo_ref.at[pl.ds(src_row, 1), :], rows.at[s, k, pl.ds(q, 1), :], sem.at[s])

    def issue(dref, s):
        def body(q, carry):
            for k in range(TOP_K):
                row_copy(dref[0, 0, q * TOP_K + k], s, k, q).start()
            return carry

        lax.fori_loop(0, tc, body, 0)

    @pl.when(t == 0)
    def _():
        issue(dcur_ref, 0)

    @pl.when(t + 1 < nt)
    def _():
        issue(dnext_ref, 1 - slot)

    def drain(q, carry):
        for k in range(TOP_K):
            row_copy(0, slot, k, 0).wait()
        return carry

    lax.fori_loop(0, tc, drain, 0)
    moe = jnp.zeros(h_ref.shape, F32)
    for k in range(TOP_K):
        moe = moe + gate_ref[:, k:k + 1] * rows[slot, k]
    y_ref[...] = h_ref[...] + moe


def _moe_combine(dest, h, gate, expert_out):
    m, d = h.shape
    tc = min(m, 128)
    assert m % tc == 0
    nt = m // tc
    dest3 = dest.reshape(nt, 1, tc * TOP_K)
    return pl.pallas_call(
        functools.partial(_moe_combine_kernel, tc=tc, nt=nt),
        grid=(nt,),
        in_specs=[pl.BlockSpec((1, 1, tc * TOP_K), lambda i: (i, 0, 0), memory_space=pltpu.SMEM),
                  pl.BlockSpec((1, 1, tc * TOP_K), lambda i: (jnp.minimum(i + 1, nt - 1), 0, 0),
                               memory_space=pltpu.SMEM),
                  pl.BlockSpec((tc, d), lambda i: (i, 0)),
                  pl.BlockSpec((tc, LANE), lambda i: (i, 0)),
                  pl.BlockSpec(memory_space=pl.ANY)],
        out_specs=pl.BlockSpec((tc, d), lambda i: (i, 0)),
        out_shape=jax.ShapeDtypeStruct((m, d), F32),
        scratch_shapes=[pltpu.VMEM((2, TOP_K, tc, d), F32), pltpu.SemaphoreType.DMA((2,))],
        compiler_params=_params(1),
        name="moe_combine",
    )(dest3, dest3, h, gate, expert_out)


def _moe_tables(idx, n_exp, n_blocks):
    r_blk = MOE_BLOCK_ROWS
    e_flat = idx.reshape(-1)
    n_asg = e_flat.shape[0]
    onehot = (e_flat[:, None] == jnp.arange(n_exp, dtype=I32)[None, :]).astype(I32)
    csum = jnp.cumsum(onehot, axis=0)
    counts = csum[-1]
    pos = jnp.take_along_axis(csum, e_flat[:, None], axis=1)[:, 0] - 1
    nblk = (counts + r_blk - 1) // r_blk
    blk_end = jnp.cumsum(nblk)
    first = blk_end - nblk
    used = blk_end[-1]
    slot = (first[e_flat] + pos // r_blk) * r_blk + pos % r_blk
    src = jnp.zeros((n_blocks * r_blk,), I32).at[slot].set(jnp.arange(n_asg, dtype=I32) // TOP_K)
    bid = jnp.minimum(jnp.arange(n_blocks, dtype=I32), used - 1)
    exp = jnp.minimum(jnp.sum((blk_end[None, :] <= bid[:, None]).astype(I32), axis=1), n_exp - 1)
    valid = jnp.clip(counts[exp] - (bid - first[exp]) * r_blk, 0, r_blk).astype(I32)
    valid = jnp.where(jnp.arange(n_blocks) < used, valid, 0)
    meta = used.reshape(1).astype(I32)
    return meta, bid, exp, valid, src.reshape(n_blocks, 1, r_blk), slot.astype(I32)


def kernel(x_prompt, x_sample, cache_k, cache_v, cache_mem_k, cache_mem_v, state_conv, page_table,
           mem_prompt, norm_mix_g, w_in, q_norm_g, k_norm_g, w_attn_o, conv_dw_w, conv_dw_b,
           conv_ln_g, conv_ln_b, w_conv_o, mem_norm_g, w_mem_kv, mq_norm_g, mk_norm_g, w_mem_o,
           w_out, norm_ffn_g, w_router, b_router, w_gu, b_gu, w_down, b_down):
    depth = w_in.shape[0]
    assert depth == 1 and x_sample.shape[1] == 1
    bp, sp, d = x_prompt.shape
    bs = x_sample.shape[0]
    ch = conv_dw_w.shape[-1]
    hd = q_norm_g.shape[-1]
    n_kv = cache_k.shape[3]
    kvw = n_kv * hd
    qw = w_attn_o.shape[1]
    n_heads = qw // hd
    mqw = w_mem_o.shape[1]
    mem_heads, mhd = cache_mem_k.shape[3], cache_mem_k.shape[4]
    n_mem = cache_mem_k.shape[2]
    n_exp = w_router.shape[-1]
    o_ub, o_q = ch, 2 * ch
    o_k = o_q + qw
    o_v = o_k + kvw
    o_qm = o_v + kvw
    o_g = o_qm + mqw
    n_in = w_in.shape[-1]
    assert n_in == o_g + 3 * d and o_ub == ch
    slopes = jnp.asarray(2.0 ** (-8.0 * np.arange(1, n_heads + 1) / n_heads), dtype=F32)
    l = 0
    tp = bp * sp

    xp = x_prompt.reshape(tp, d)
    xn_p = _rmsnorm(xp, norm_mix_g[l], BF16)
    proj_p = _mm(xn_p, w_in[l])
    act_p, conv_p = _conv_prompt(proj_p, bp, sp, ch, conv_dw_w[l], conv_dw_b[l], conv_ln_g[l], conv_ln_b[l])
    kn_p, kmean_p = _headnorm(proj_p, o_k, kvw, k_norm_g[l], rows_per_block=MOBA_BLOCK, with_mean=True)
    attn_p = _moba_prompt(proj_p, kn_p, kmean_p.reshape(bp, sp // MOBA_BLOCK, kvw), q_norm_g[l], slopes,
                          bp, sp, o_q, o_v, n_heads, n_kv)
    memn = _rmsnorm(mem_prompt.reshape(bp * n_mem, d), mem_norm_g[l], BF16)
    mem_kv = _mm(memn, w_mem_kv[l])
    mk_p = _headnorm(mem_kv, 0, mqw, mk_norm_g[l], rows_per_block=256, with_mean=False)
    mv_p = mem_kv[:, mqw:]
    memo_p = _mem_attn(proj_p.reshape(bp, sp, n_in), o_qm, mk_p.reshape(bp, n_mem, mqw),
                       mv_p.reshape(bp, n_mem, mqw), mq_norm_g[l], hi=False, out_dtype=BF16)
    merged_p = _merge(act_p, attn_p, memo_p.reshape(tp, mqw), w_conv_o[l], w_attn_o[l], w_mem_o[l],
                      proj_p, o_g, hi=False, out_dtype=BF16)
    h_p = _mm(merged_p, w_out[l], residual=xp)
    hn_all, idx_p, gate_p = _router(h_p, norm_ffn_g[l], w_router[l], b_router[l], total_rows=tp + bs)

    xs = x_sample.reshape(bs, d)
    xn_s = _rmsnorm(xs, norm_mix_g[l], F32)
    proj_s = _mm(xn_s, w_in[l], hi=True)
    proj_s3 = proj_s.reshape(bs, 1, n_in)
    act_s, conv_s = _conv_sample(proj_s3, state_conv[l], conv_dw_w[l], conv_dw_b[l], conv_ln_g[l], conv_ln_b[l])
    kn_s = _headnorm(proj_s, o_k, kvw, k_norm_g[l], rows_per_block=bs, with_mean=False)
    v_s = proj_s[:, o_v:o_v + kvw]
    attn_s = _moba_sample(page_table, proj_s[:, o_q:o_q + qw].reshape(bs, n_heads, hd),
                          kn_s.reshape(bs, n_kv, hd), v_s.reshape(bs, n_kv, hd), q_norm_g[l], slopes,
                          cache_k[l], cache_v[l])
    memo_s = _mem_attn(proj_s3, o_qm, cache_mem_k[l].reshape(bs, n_mem, mqw),
                       cache_mem_v[l].reshape(bs, n_mem, mqw), mq_norm_g[l], hi=True, out_dtype=F32)
    merged_s = _merge(act_s.reshape(bs, ch), attn_s.reshape(bs, qw), memo_s.reshape(bs, mqw),
                      w_conv_o[l], w_attn_o[l], w_mem_o[l], proj_s, o_g, hi=True, out_dtype=F32)
    h_s = _mm(merged_s, w_out[l], residual=xs, hi=True)
    hn_all, idx_s, gate_s = _router(h_s, norm_ffn_g[l], w_router[l], b_router[l], total_rows=tp + bs,
                                    row_off=tp, hn_into=hn_all)

    n_asg = (tp + bs) * TOP_K
    n_blocks = n_asg // MOE_BLOCK_ROWS + n_exp
    idx_all = jnp.concatenate([idx_p[:, :TOP_K], idx_s[:, :TOP_K]], axis=0)
    meta, bid, exp, valid, src, slot = _moe_tables(idx_all, n_exp, n_blocks)
    x_sorted = _moe_gather(meta, bid, valid, src, hn_all, n_blocks)
    act_e = _moe_up(meta, bid, exp, valid, x_sorted, w_gu[l], b_gu[l], n_blocks)
    out_e = _moe_down(meta, bid, exp, valid, act_e, w_down[l], b_down[l], n_blocks)
    y_p = _moe_combine(slot[:tp * TOP_K], h_p, gate_p, out_e)
    y_s = _moe_combine(slot[tp * TOP_K:], h_s, gate_s, out_e)

    return (y_p.reshape(bp, sp, d), y_s.reshape(bs, 1, d),
            kn_p.reshape(depth, bp, sp, n_kv, hd),
            proj_p[:, o_v:o_v + kvw].reshape(depth, bp, sp, n_kv, hd),
            kn_s.reshape(depth, bs, 1, n_kv, hd), v_s.reshape(depth, bs, 1, n_kv, hd),
            mk_p.reshape(depth, bp, n_mem, mem_heads, mhd), mv_p.reshape(depth, bp, n_mem, mem_heads, mhd),
            conv_p.reshape(depth, bp, conv_p.shape[1], ch), conv_s.reshape(depth, bs, conv_s.shape[1], ch))
```
